```python
import jax, jax.numpy as jnp
from jax import lax
import numpy as np


D_MODEL = 2048
BATCH = 2
SEQ = 16384
DEPTH = 1

N_FOX_HEADS = 8
FOX_HEAD_DIM = D_MODEL // 16
FOX_WIDTH = N_FOX_HEADS * FOX_HEAD_DIM
N_MLSTM_HEADS = 4
MLSTM_V_DIM = D_MODEL // 8
MLSTM_QK_DIM = MLSTM_V_DIM // 2
MLSTM_WIDTH = N_MLSTM_HEADS * MLSTM_V_DIM
MLSTM_QK_WIDTH = N_MLSTM_HEADS * MLSTM_QK_DIM
MIX_WIDTH = FOX_WIDTH + MLSTM_WIDTH
CONV_WIDTH = 4
D_FF = ((8 * D_MODEL // 3 + 255) // 256) * 256
Q_BLOCK = 128
MLSTM_CHUNK = 64
N_ADA = 9
ALPHA = (2 * DEPTH) ** 0.25
BETA = (8 * DEPTH) ** -0.25
LN_EPS = 1e-5

COL_FOX_Q = 0
COL_FOX_K = COL_FOX_Q + FOX_WIDTH
COL_FOX_V = COL_FOX_K + FOX_WIDTH
COL_FOX_F = COL_FOX_V + FOX_WIDTH
COL_MLSTM_Q = COL_FOX_F + N_FOX_HEADS
COL_MLSTM_K = COL_MLSTM_Q + MLSTM_QK_WIDTH
COL_MLSTM_V = COL_MLSTM_K + MLSTM_QK_WIDTH
COL_MLSTM_I = COL_MLSTM_V + MLSTM_WIDTH
COL_MLSTM_F = COL_MLSTM_I + N_MLSTM_HEADS
COL_MLSTM_O = COL_MLSTM_F + N_MLSTM_HEADS
IN_WIDTH = COL_MLSTM_O + MLSTM_WIDTH

kernel_name = 'hymba_fox_mlstm_macaron_deepnorm_adaln'


def _layernorm(x, g=None, b=None):
    xf = x.astype(jnp.float32)
    mu = xf.mean(-1, keepdims=True)
    var = jnp.square(xf - mu).mean(-1, keepdims=True)
    y = (xf - mu) * lax.rsqrt(var + LN_EPS)
    if g is not None:
        y = y * g + b
    return y.astype(x.dtype)


def _modulate(x, shift, scale):
    return _layernorm(x) * (1 + scale) + shift


def _swiglu(h, w_in, w_out):
    gate, up = jnp.split(h @ w_in, 2, axis=-1)
    return (jax.nn.silu(gate) * up) @ w_out


def _fox_attention(q, k, v, log_f):
    B, S = q.shape[:2]
    scale = FOX_HEAD_DIM ** -0.5
    qh = jnp.swapaxes(q, 1, 2)
    kh = jnp.swapaxes(k, 1, 2)
    vh = jnp.swapaxes(v, 1, 2)
    F = jnp.cumsum(jnp.swapaxes(log_f, 1, 2), axis=-1)
    key_pos = jnp.arange(S)

    def block(i):
        start = i * Q_BLOCK
        qb = lax.dynamic_slice_in_dim(qh, start, Q_BLOCK, axis=2)
        Fq = lax.dynamic_slice_in_dim(F, start, Q_BLOCK, axis=2)
        logits = (jnp.einsum('bhqd,bhkd->bhqk', qb, kh).astype(jnp.float32) * scale
                  + Fq[..., :, None] - F[..., None, :])
        q_pos = start + jnp.arange(Q_BLOCK)
        mask = key_pos[None, :] <= q_pos[:, None]
        p = jax.nn.softmax(jnp.where(mask, logits, -jnp.inf), axis=-1).astype(vh.dtype)
        return jnp.einsum('bhqk,bhkd->bhqd', p, vh)

    out = lax.map(block, jnp.arange(S // Q_BLOCK))
    return jnp.transpose(out, (1, 0, 3, 2, 4)).reshape(B, S, FOX_WIDTH)


def _causal_dwconv(u, w, b):
    S = u.shape[1]
    up = jnp.pad(u, ((0, 0), (CONV_WIDTH - 1, 0), (0, 0)))
    return sum(w[j] * up[:, j:j + S] for j in range(CONV_WIDTH)) + b


def _mlstm(q, k, v, i_pre, log_f):
    B, S, H, dk = q.shape
    dv = v.shape[-1]
    L = MLSTM_CHUNK
    nc = S // L

    def chunks4(a):
        return jnp.transpose(a.reshape(B, nc, L, H, a.shape[-1]), (1, 0, 3, 2, 4))

    def chunks3(a):
        return jnp.transpose(a.reshape(B, nc, L, H), (1, 0, 3, 2))

    causal = jnp.tril(jnp.ones((L, L), dtype=bool))

    def step(carry, xs):
        C, n, m = carry
        qc, kc, vc, ic, fc = xs
        b = jnp.cumsum(fc, axis=-1)
        dlog = jnp.where(causal, b[..., :, None] - b[..., None, :] + ic[..., None, :], -jnp.inf)
        m_inter = b + m[..., None]
        m_t = jnp.maximum(m_inter, dlog.max(-1))
        s = jnp.einsum('bhtd,bhsd->bhts', qc, kc).astype(jnp.float32) * jnp.exp(dlog - m_t[..., None])
        inter = jnp.exp(m_inter - m_t)
        num = (inter[..., None] * jnp.einsum('bhtd,bhdv->bhtv', qc, C)
               + jnp.einsum('bhts,bhsv->bhtv', s, vc))
        den = inter * jnp.einsum('bhtd,bhd->bht', qc, n) + s.sum(-1)
        h = num / jnp.maximum(jnp.abs(den), jnp.exp(-m_t))[..., None]
        bL = b[..., -1]
        wlog = bL[..., None] - b + ic
        m_new = jnp.maximum(bL + m, wlog.max(-1))
        decay = jnp.exp(bL + m - m_new)
        wk = kc * jnp.exp(wlog - m_new[..., None])[..., None]
        C_new = decay[..., None, None] * C + jnp.einsum('bhsd,bhsv->bhdv', wk, vc)
        n_new = decay[..., None] * n + wk.sum(2)
        return (C_new, n_new, m_new), h

    init = (jnp.zeros((B, H, dk, dv), jnp.float32),
            jnp.zeros((B, H, dk), jnp.float32),
            jnp.zeros((B, H), jnp.float32))
    xs = (chunks4(q), chunks4(k), chunks4(v), chunks3(i_pre), chunks3(log_f))
    _, h = lax.scan(step, init, xs)
    return jnp.transpose(h, (1, 0, 3, 2, 4)).reshape(B, S, H, dv)


def _token_mixer(h, w_in, fox_f_bias, mlstm_conv_w, mlstm_conv_b, mlstm_i_bias,
                 mlstm_f_bias, mlstm_norm_g, w_out):
    B, S, _ = h.shape
    z = h @ w_in
    fq = z[..., COL_FOX_Q:COL_FOX_K].reshape(B, S, N_FOX_HEADS, FOX_HEAD_DIM)
    fk = z[..., COL_FOX_K:COL_FOX_V].reshape(B, S, N_FOX_HEADS, FOX_HEAD_DIM)
    fv = z[..., COL_FOX_V:COL_FOX_F].reshape(B, S, N_FOX_HEADS, FOX_HEAD_DIM)
    fox_logf = jax.nn.log_sigmoid((z[..., COL_FOX_F:COL_MLSTM_Q] + fox_f_bias).astype(jnp.float32))
    y_fox = _fox_attention(fq, fk, fv, fox_logf)
    qk = jax.nn.silu(_causal_dwconv(z[..., COL_MLSTM_Q:COL_MLSTM_V], mlstm_conv_w, mlstm_conv_b))
    mq = qk[..., :MLSTM_QK_WIDTH].reshape(B, S, N_MLSTM_HEADS, MLSTM_QK_DIM)
    mk = qk[..., MLSTM_QK_WIDTH:].reshape(B, S, N_MLSTM_HEADS, MLSTM_QK_DIM) * (MLSTM_QK_DIM ** -0.5)
    mv = z[..., COL_MLSTM_V:COL_MLSTM_I].reshape(B, S, N_MLSTM_HEADS, MLSTM_V_DIM)
    m_i = (z[..., COL_MLSTM_I:COL_MLSTM_F] + mlstm_i_bias).astype(jnp.float32)
    m_logf = jax.nn.log_sigmoid((z[..., COL_MLSTM_F:COL_MLSTM_O] + mlstm_f_bias).astype(jnp.float32))
    hm = _mlstm(mq, mk, mv, m_i, m_logf)
    hm = hm * lax.rsqrt(jnp.square(hm).mean(-1, keepdims=True) + LN_EPS)
    hm = hm * mlstm_norm_g.reshape(N_MLSTM_HEADS, MLSTM_V_DIM)
    o = jax.nn.sigmoid(z[..., COL_MLSTM_O:IN_WIDTH]).reshape(B, S, N_MLSTM_HEADS, MLSTM_V_DIM)
    y_mlstm = (o * hm).reshape(B, S, MLSTM_WIDTH).astype(h.dtype)
    y = jnp.concatenate([y_fox, y_mlstm], axis=-1)
    return y @ w_out


def _layer(x, c, w_ada, b_ada, ffn1_w_in, ffn1_w_out, ln1_g, ln1_b, w_in, fox_f_bias,
           mlstm_conv_w, mlstm_conv_b, mlstm_i_bias, mlstm_f_bias, mlstm_norm_g, w_out,
           ln2_g, ln2_b, ffn2_w_in, ffn2_w_out, ln3_g, ln3_b):
    ada = (jax.nn.silu(c) @ w_ada + b_ada)[:, None, :]
    sh1, sc1, g1, sh2, sc2, g2, sh3, sc3, g3 = jnp.split(ada, N_ADA, axis=-1)
    h = _swiglu(_modulate(x, sh1, sc1), ffn1_w_in, ffn1_w_out)
    x = _layernorm(ALPHA * x + 0.5 * (1 + g1) * h, ln1_g, ln1_b)
    h = _token_mixer(_modulate(x, sh2, sc2), w_in, fox_f_bias, mlstm_conv_w, mlstm_conv_b,
                     mlstm_i_bias, mlstm_f_bias, mlstm_norm_g, w_out)
    x = _layernorm(ALPHA * x + (1 + g2) * h, ln2_g, ln2_b)
    h = _swiglu(_modulate(x, sh3, sc3), ffn2_w_in, ffn2_w_out)
    x = _layernorm(ALPHA * x + 0.5 * (1 + g3) * h, ln3_g, ln3_b)
    return x


def setup_inputs(seed: int = 0) -> dict:
    key = jax.random.key(seed)
    ks = jax.random.split(key, 24)
    f32 = jnp.float32

    def nrm(k, shape, s):
        return jax.random.normal(k, shape, f32) * s

    D = D_MODEL
    return {
        'x': nrm(ks[0], (BATCH, SEQ, D), 1.0),
        'c': nrm(ks[1], (BATCH, D), 1.0),
        'w_ada': nrm(ks[2], (DEPTH, D, N_ADA * D), 0.1 * D ** -0.5),
        'b_ada': nrm(ks[3], (DEPTH, N_ADA * D), 0.01),
        'ffn1_w_in': nrm(ks[4], (DEPTH, D, 2 * D_FF), D ** -0.5),
        'ffn1_w_out': nrm(ks[5], (DEPTH, D_FF, D), BETA * D_FF ** -0.5),
        'ln1_g': 1.0 + nrm(ks[6], (DEPTH, D), 0.05),
        'ln1_b': nrm(ks[7], (DEPTH, D), 0.02),
        'w_in': nrm(ks[8], (DEPTH, D, IN_WIDTH), D ** -0.5),
        'fox_f_bias': jnp.linspace(1.0, 6.0, N_FOX_HEADS, dtype=f32)[None, :] + nrm(ks[9], (DEPTH, N_FOX_HEADS), 0.1),
        'mlstm_conv_w': nrm(ks[10], (DEPTH, CONV_WIDTH, 2 * MLSTM_QK_WIDTH), CONV_WIDTH ** -0.5),
        'mlstm_conv_b': nrm(ks[11], (DEPTH, 2 * MLSTM_QK_WIDTH), 0.02),
        'mlstm_i_bias': nrm(ks[12], (DEPTH, N_MLSTM_HEADS), 0.1),
        'mlstm_f_bias': jnp.linspace(3.0, 6.0, N_MLSTM_HEADS, dtype=f32)[None, :] + nrm(ks[13], (DEPTH, N_MLSTM_HEADS), 0.1),
        'mlstm_norm_g': 1.0 + nrm(ks[14], (DEPTH, MLSTM_WIDTH), 0.05),
        'w_out': nrm(ks[15], (DEPTH, MIX_WIDTH, D), BETA * MIX_WIDTH ** -0.5),
        'ln2_g': 1.0 + nrm(ks[16], (DEPTH, D), 0.05),
        'ln2_b': nrm(ks[17], (DEPTH, D), 0.02),
        'ffn2_w_in': nrm(ks[18], (DEPTH, D, 2 * D_FF), D ** -0.5),
        'ffn2_w_out': nrm(ks[19], (DEPTH, D_FF, D), BETA * D_FF ** -0.5),
        'ln3_g': 1.0 + nrm(ks[20], (DEPTH, D), 0.05),
        'ln3_b': nrm(ks[21], (DEPTH, D), 0.02),
    }


def reference(x, c, w_ada, b_ada, ffn1_w_in, ffn1_w_out, ln1_g, ln1_b, w_in, fox_f_bias,
              mlstm_conv_w, mlstm_conv_b, mlstm_i_bias, mlstm_f_bias, mlstm_norm_g, w_out,
              ln2_g, ln2_b, ffn2_w_in, ffn2_w_out, ln3_g, ln3_b):
    for l in range(DEPTH):
        x = _layer(x, c, w_ada[l], b_ada[l], ffn1_w_in[l], ffn1_w_out[l], ln1_g[l], ln1_b[l],
                   w_in[l], fox_f_bias[l], mlstm_conv_w[l], mlstm_conv_b[l], mlstm_i_bias[l],
                   mlstm_f_bias[l], mlstm_norm_g[l], w_out[l], ln2_g[l], ln2_b[l],
                   ffn2_w_in[l], ffn2_w_out[l], ln3_g[l], ln3_b[l])
    return x
```

```python
import functools

import jax
import jax.numpy as jnp
from jax import lax
from jax.experimental import pallas as pl
from jax.experimental.pallas import tpu as pltpu

F32 = jnp.float32
BF16 = jnp.bfloat16

D_MODEL = 2048
DEPTH = 1
N_FOX_HEADS = 8
FOX_HEAD_DIM = 128
FOX_WIDTH = N_FOX_HEADS * FOX_HEAD_DIM
N_MLSTM_HEADS = 4
MLSTM_V_DIM = 256
MLSTM_QK_DIM = 128
MLSTM_WIDTH = N_MLSTM_HEADS * MLSTM_V_DIM
MLSTM_QK_WIDTH = N_MLSTM_HEADS * MLSTM_QK_DIM
CONV_WIDTH = 4
D_FF = 5632
N_ADA = 9
ALPHA = (2 * DEPTH) ** 0.25
LN_EPS = 1e-5

COL_FOX_F = 3 * FOX_WIDTH
COL_MLSTM_Q = COL_FOX_F + N_FOX_HEADS
COL_MLSTM_I = COL_MLSTM_Q + 2 * MLSTM_QK_WIDTH + MLSTM_WIDTH
COL_MLSTM_O = COL_MLSTM_I + 2 * N_MLSTM_HEADS
IN_WIDTH = COL_MLSTM_O + MLSTM_WIDTH

GATE_LANES = 128
G_FOX = 0
G_MI = N_FOX_HEADS
G_MF = G_MI + N_MLSTM_HEADS
G_ROWS = 16
MIX_COLS = 3 * FOX_WIDTH + 2 * MLSTM_QK_WIDTH + 2 * MLSTM_WIDTH

MLSTM_CHUNK = 256
NEG_BIG = -1e30
VMEM_LIMIT = 56 * 1024 * 1024


def _cparams(sem):
    return pltpu.CompilerParams(dimension_semantics=sem, vmem_limit_bytes=VMEM_LIMIT)


def _ln(x):
    mu = jnp.mean(x, axis=-1, keepdims=True)
    xc = x - mu
    var = jnp.mean(xc * xc, axis=-1, keepdims=True)
    return xc * lax.rsqrt(var + LN_EPS)


def _sigmoid(x):
    return 1.0 / (1.0 + jnp.exp(-x))


def _log_sigmoid(x):
    return jnp.minimum(x, 0.0) - jnp.log1p(jnp.exp(-jnp.abs(x)))


def _ada_kernel(c_ref, w_ref, b_ref, o_ref):
    c = c_ref[...]
    s = (c * _sigmoid(c)).astype(BF16)
    o_ref[...] = jnp.dot(s, w_ref[...].astype(BF16), preferred_element_type=F32) + b_ref[...]


def _ada(c_pad, w_ada, b_ada):
    rows, d = c_pad.shape
    n = w_ada.shape[1]
    tn = 1024
    return pl.pallas_call(
        _ada_kernel,
        grid=(n // tn,),
        in_specs=[pl.BlockSpec((rows, d), lambda j: (0, 0)),
                  pl.BlockSpec((d, tn), lambda j: (0, j)),
                  pl.BlockSpec((1, tn), lambda j: (0, j))],
        out_specs=pl.BlockSpec((rows, tn), lambda j: (0, j)),
        out_shape=jax.ShapeDtypeStruct((rows, n), F32),
        compiler_params=_cparams(("arbitrary",)),
        name="ada_proj",
    )(c_pad, w_ada, b_ada)


def _ffn_kernel(x_ref, sh_ref, sc_ref, gt_ref, wg_ref, wu_ref, wo_ref, lng_ref, lnb_ref, o_ref, h_ref):
    k = pl.program_id(2)

    @pl.when(k == 0)
    def _():
        h = _ln(x_ref[0]) * (1.0 + sc_ref[0]) + sh_ref[0]
        h_ref[...] = h.astype(BF16)

    h = h_ref[...]
    g = jnp.dot(h, wg_ref[...], preferred_element_type=F32)
    u = jnp.dot(h, wu_ref[...], preferred_element_type=F32)
    a = (g * _sigmoid(g) * u).astype(BF16)
    part = jnp.dot(a, wo_ref[...], preferred_element_type=F32)

    @pl.when(k == 0)
    def _():
        o_ref[0] = part

    @pl.when(k > 0)
    def _():
        o_ref[0] += part

    @pl.when(k == pl.num_programs(2) - 1)
    def _():
        y = ALPHA * x_ref[0] + (0.5 * (1.0 + gt_ref[0])) * o_ref[0]
        o_ref[0] = _ln(y) * lng_ref[...] + lnb_ref[...]


def _ffn(x, ada3, idx, w_in, w_out, ln_g, ln_b):
    b, s, d = x.shape
    dff = w_out.shape[0]
    tm = min(512, s)
    tf = 512
    nk = dff // tf
    vec = lambda j: pl.BlockSpec((1, 1, d), lambda bi, i, k: (bi, 0, j))
    return pl.pallas_call(
        _ffn_kernel,
        grid=(b, s // tm, nk),
        in_specs=[pl.BlockSpec((1, tm, d), lambda bi, i, k: (bi, i, 0)),
                  vec(idx), vec(idx + 1), vec(idx + 2),
                  pl.BlockSpec((d, tf), lambda bi, i, k: (0, k)),
                  pl.BlockSpec((d, tf), lambda bi, i, k: (0, k + nk)),
                  pl.BlockSpec((tf, d), lambda bi, i, k: (k, 0)),
                  pl.BlockSpec((1, d), lambda bi, i, k: (0, 0)),
                  pl.BlockSpec((1, d), lambda bi, i, k: (0, 0))],
        out_specs=pl.BlockSpec((1, tm, d), lambda bi, i, k: (bi, i, 0)),
        out_shape=jax.ShapeDtypeStruct((b, s, d), F32),
        scratch_shapes=[pltpu.VMEM((tm, d), BF16)],
        compiler_params=_cparams(("parallel", "parallel", "arbitrary")),
        name="ffn",
    )(x, ada3, ada3, ada3, w_in, w_in, w_out, ln_g, ln_b)


def _inproj_kernel(x_ref, sh_ref, sc_ref, w_ref, wgate_ref, z_ref, zg_ref, h_ref, *, n_scaled, q_scale):
    j = pl.program_id(2)

    @pl.when(j == 0)
    def _():
        h = (_ln(x_ref[0]) * (1.0 + sc_ref[0]) + sh_ref[0]).astype(BF16)
        h_ref[...] = h
        zg_ref[0] = jnp.dot(h, wgate_ref[...], preferred_element_type=F32)

    z = jnp.dot(h_ref[...], w_ref[...], preferred_element_type=F32)
    z = z * jnp.where(j < n_scaled, q_scale, 1.0).astype(F32)
    z_ref[0] = z.astype(BF16)


def _inproj(x, ada3, idx, w_mix, w_gate):
    b, s, d = x.shape
    n = w_mix.shape[1]
    tm = min(512, s)
    tn = 512
    vec = lambda j: pl.BlockSpec((1, 1, d), lambda bi, i, jj: (bi, 0, j))
    kern = functools.partial(_inproj_kernel, n_scaled=FOX_WIDTH // tn, q_scale=FOX_HEAD_DIM ** -0.5)
    return pl.pallas_call(
        kern,
        grid=(b, s // tm, n // tn),
        in_specs=[pl.BlockSpec((1, tm, d), lambda bi, i, j: (bi, i, 0)),
                  vec(idx), vec(idx + 1),
                  pl.BlockSpec((d, tn), lambda bi, i, j: (0, j)),
                  pl.BlockSpec((d, GATE_LANES), lambda bi, i, j: (0, 0))],
        out_specs=[pl.BlockSpec((1, tm, tn), lambda bi, i, j: (bi, i, j)),
                   pl.BlockSpec((1, tm, GATE_LANES), lambda bi, i, j: (bi, i, 0))],
        out_shape=[jax.ShapeDtypeStruct((b, s, n), BF16),
                   jax.ShapeDtypeStruct((b, s, GATE_LANES), F32)],
        scratch_shapes=[pltpu.VMEM((tm, d), BF16)],
        compiler_params=_cparams(("parallel", "parallel", "arbitrary")),
        name="in_proj",
    )(x, ada3, ada3, w_mix, w_gate)


def _split3(v):
    hi = v.astype(BF16)
    r = v - hi.astype(F32)
    mid = r.astype(BF16)
    lo = (r - mid.astype(F32)).astype(BF16)
    return hi, mid, lo


def _gate_kernel(zg_ref, bias_ref, nat_ref, t_ref, carry_ref):
    i = pl.program_id(1)

    @pl.when(i == 0)
    def _():
        carry_ref[...] = jnp.zeros_like(carry_ref)

    z = zg_ref[0] + bias_ref[...]
    rows = z.shape[0]
    col = lax.broadcasted_iota(jnp.int32, z.shape, 1)
    is_in_gate = (col >= G_MI) & (col < G_MF)
    v = jnp.where(is_in_gate, z, _log_sigmoid(z))
    r_i = lax.broadcasted_iota(jnp.int32, (rows, rows), 0)
    c_i = lax.broadcasted_iota(jnp.int32, (rows, rows), 1)
    tri = jnp.where(r_i >= c_i, 1.0, 0.0).astype(BF16)
    hi, mid, lo = _split3(v)
    csum = (jnp.dot(tri, hi, preferred_element_type=F32)
            + jnp.dot(tri, mid, preferred_element_type=F32)
            + jnp.dot(tri, lo, preferred_element_type=F32))
    running = csum + carry_ref[...]
    carry_ref[...] = running[rows - 1:rows, :]
    out = jnp.where(col < G_MI, running, jnp.where(is_in_gate, v, csum))
    nat_ref[0] = out
    t_ref[0] = out.T[:G_ROWS, :]


def _gates(zg, bias):
    b, s, _ = zg.shape
    tl = min(MLSTM_CHUNK, s)
    return pl.pallas_call(
        _gate_kernel,
        grid=(b, s // tl),
        in_specs=[pl.BlockSpec((1, tl, GATE_LANES), lambda bi, i: (bi, i, 0)),
                  pl.BlockSpec((1, GATE_LANES), lambda bi, i: (0, 0))],
        out_specs=[pl.BlockSpec((1, tl, GATE_LANES), lambda bi, i: (bi, i, 0)),
                   pl.BlockSpec((1, G_ROWS, tl), lambda bi, i: (bi, 0, i))],
        out_shape=[jax.ShapeDtypeStruct((b, s, GATE_LANES), F32),
                   jax.ShapeDtypeStruct((b, G_ROWS, s), F32)],
        scratch_shapes=[pltpu.VMEM((1, GATE_LANES), F32)],
        compiler_params=_cparams(("parallel", "arbitrary")),
        name="gates",
    )(zg, bias)


def _fox_kernel(q_ref, k_ref, v_ref, f_ref, o_ref, *, tq):
    qi = pl.program_id(2)
    q = q_ref[0]

    def step(j, carry, masked):
        m, l, acc = carry
        start = pl.multiple_of(j * tq, tq)
        k = k_ref[0, pl.ds(start, tq), :]
        v = v_ref[0, pl.ds(start, tq), :]
        s = lax.dot_general(q, k, (((1,), (1,)), ((), ())), preferred_element_type=F32)
        s = s - f_ref[0, 0, j]
        if masked:
            r_i = lax.broadcasted_iota(jnp.int32, s.shape, 0)
            c_i = lax.broadcasted_iota(jnp.int32, s.shape, 1)
            s = jnp.where(c_i <= r_i, s, NEG_BIG)
        m_new = jnp.maximum(m, jnp.max(s, axis=-1, keepdims=True))
        p = jnp.exp(s - m_new)
        a = jnp.exp(m - m_new)
        l = a * l + jnp.sum(p, axis=-1, keepdims=True)
        acc = a * acc + jnp.dot(p.astype(BF16), v, preferred_element_type=F32)
        return m_new, l, acc

    init = (jnp.full((tq, 1), NEG_BIG, F32), jnp.zeros((tq, 1), F32), jnp.zeros((tq, FOX_HEAD_DIM), F32))
    carry = lax.fori_loop(0, qi, lambda j, c: step(j, c, False), init)
    _, l, acc = step(qi, carry, True)
    o_ref[0] = (acc / l).astype(o_ref.dtype)


def _fox(z, gates_t):
    b, s, _ = z.shape
    tq = min(512, s)
    nb = s // tq
    f5 = gates_t.reshape(b, G_ROWS, nb, 1, tq)
    h = N_FOX_HEADS
    return pl.pallas_call(
        functools.partial(_fox_kernel, tq=tq),
        grid=(b, h, nb),
        in_specs=[pl.BlockSpec((1, tq, FOX_HEAD_DIM), lambda bi, hi, qi: (bi, qi, hi)),
                  pl.BlockSpec((1, s, FOX_HEAD_DIM), lambda bi, hi, qi: (bi, 0, h + hi)),
                  pl.BlockSpec((1, s, FOX_HEAD_DIM), lambda bi, hi, qi: (bi, 0, 2 * h + hi)),
                  pl.BlockSpec((1, 1, nb, 1, tq), lambda bi, hi, qi: (bi, G_FOX + hi, 0, 0, 0))],
        out_specs=pl.BlockSpec((1, tq, FOX_HEAD_DIM), lambda bi, hi, qi: (bi, qi, hi)),
        out_shape=jax.ShapeDtypeStruct((b, s, FOX_WIDTH), BF16),
        compiler_params=_cparams(("parallel", "parallel", "arbitrary")),
        name="fox_attn",
    )(z, z, z, f5)


def _mlstm_kernel(qk_ref, v_ref, o_ref, gn_ref, gt_ref, cw_ref, cb_ref, ng_ref, y_ref,
                  c_st, n_st, m_st, tail, ubuf):
    c = pl.program_id(1)
    L = qk_ref.shape[1]
    dk, dv = MLSTM_QK_DIM, MLSTM_V_DIM

    @pl.when(c == 0)
    def _():
        c_st[...] = jnp.zeros_like(c_st)
        n_st[...] = jnp.zeros_like(n_st)
        m_st[...] = jnp.zeros_like(m_st)
        tail[...] = jnp.zeros_like(tail)

    u = qk_ref[0].astype(F32)
    ubuf[0:8, :] = tail[...]
    ubuf[8:8 + L, :] = u
    tail[...] = u[L - 8:L, :]
    conv = cb_ref[...] + cw_ref[3:4, :] * u
    for d in range(1, CONV_WIDTH):
        conv = conv + cw_ref[3 - d:4 - d, :] * ubuf[8 - d:8 - d + L, :]
    qk = conv * _sigmoid(conv)

    gn = gn_ref[0]
    gt = gt_ref[0]
    r_i = lax.broadcasted_iota(jnp.int32, (L, L), 0)
    c_i = lax.broadcasted_iota(jnp.int32, (L, L), 1)
    causal = c_i <= r_i

    for h in range(N_MLSTM_HEADS):
        qf = qk[:, h * dk:(h + 1) * dk]
        q = qf.astype(BF16)
        kf = qk[:, MLSTM_QK_WIDTH + h * dk:MLSTM_QK_WIDTH + (h + 1) * dk] * (dk ** -0.5)
        v = v_ref[0, :, h * dv:(h + 1) * dv]
        bcol = gn[:, G_MF + h:G_MF + h + 1]
        icol = gn[:, G_MI + h:G_MI + h + 1]
        brow = gt[G_MF + h:G_MF + h + 1, :]
        irow = gt[G_MI + h:G_MI + h + 1, :]
        m_prev = m_st[h][:, 0:1]
        c_prev = c_st[h]
        n_prev = n_st[h]

        m_inter = bcol + m_prev
        dlog = jnp.where(causal, bcol - brow + irow, NEG_BIG)
        m_t = jnp.maximum(m_inter, jnp.max(dlog, axis=-1, keepdims=True))
        s = lax.dot_general(q, kf.astype(BF16), (((1,), (1,)), ((), ())), preferred_element_type=F32)
        s = s * jnp.exp(dlog - m_t)
        inter = jnp.exp(m_inter - m_t)
        num = (inter * jnp.dot(q, c_prev.astype(BF16), preferred_element_type=F32)
               + jnp.dot(s.astype(BF16), v, preferred_element_type=F32))
        den = inter * jnp.sum(qf * n_prev, axis=-1, keepdims=True) + jnp.sum(s, axis=-1, keepdims=True)
        hh = num / jnp.maximum(jnp.abs(den), jnp.exp(-m_t))

        b_last = bcol[L - 1:L, :]
        wlog = b_last - bcol + icol
        m_new = jnp.maximum(b_last + m_prev, jnp.max(wlog, axis=0, keepdims=True))
        decay = jnp.exp(b_last + m_prev - m_new)
        wk = kf * jnp.exp(wlog - m_new)
        c_st[h] = decay * c_prev + jnp.dot(wk.T.astype(BF16), v, preferred_element_type=F32)
        n_st[h] = decay * n_prev + jnp.sum(wk, axis=0, keepdims=True)
        m_st[h] = jnp.broadcast_to(m_new, (1, GATE_LANES))

        hn = hh * lax.rsqrt(jnp.mean(hh * hh, axis=-1, keepdims=True) + LN_EPS)
        hn = hn * ng_ref[:, h * dv:(h + 1) * dv]
        og = _sigmoid(o_ref[0, :, h * dv:(h + 1) * dv].astype(F32))
        y_ref[0, :, h * dv:(h + 1) * dv] = (og * hn).astype(y_ref.dtype)


def _mlstm(z, gates_n, gates_t, conv_w, conv_b, norm_g):
    b, s, _ = z.shape
    L = min(MLSTM_CHUNK, s)
    wq = 2 * MLSTM_QK_WIDTH
    assert wq == MLSTM_WIDTH == FOX_WIDTH
    base = 3 * FOX_WIDTH // wq
    return pl.pallas_call(
        _mlstm_kernel,
        grid=(b, s // L),
        in_specs=[pl.BlockSpec((1, L, wq), lambda bi, ci: (bi, ci, base)),
                  pl.BlockSpec((1, L, MLSTM_WIDTH), lambda bi, ci: (bi, ci, base + 1)),
                  pl.BlockSpec((1, L, MLSTM_WIDTH), lambda bi, ci: (bi, ci, base + 2)),
                  pl.BlockSpec((1, L, GATE_LANES), lambda bi, ci: (bi, ci, 0)),
                  pl.BlockSpec((1, G_ROWS, L), lambda bi, ci: (bi, 0, ci)),
                  pl.BlockSpec((CONV_WIDTH, wq), lambda bi, ci: (0, 0)),
                  pl.BlockSpec((1, wq), lambda bi, ci: (0, 0)),
                  pl.BlockSpec((1, MLSTM_WIDTH), lambda bi, ci: (0, 0))],
        out_specs=pl.BlockSpec((1, L, MLSTM_WIDTH), lambda bi, ci: (bi, ci, 0)),
        out_shape=jax.ShapeDtypeStruct((b, s, MLSTM_WIDTH), BF16),
        scratch_shapes=[pltpu.VMEM((N_MLSTM_HEADS, MLSTM_QK_DIM, MLSTM_V_DIM), F32),
                        pltpu.VMEM((N_MLSTM_HEADS, 1, MLSTM_QK_DIM), F32),
                        pltpu.VMEM((N_MLSTM_HEADS, 1, GATE_LANES), F32),
                        pltpu.VMEM((8, wq), F32),
                        pltpu.VMEM((L + 8, wq), F32)],
        compiler_params=_cparams(("parallel", "arbitrary")),
        name="mlstm",
    )(z, z, z, gates_n, gates_t, conv_w, conv_b, norm_g)


def _outproj_kernel(yf_ref, ym_ref, x_ref, gt_ref, w_ref, lng_ref, lnb_ref, o_ref):
    nf = yf_ref.shape[2]
    hmix = (jnp.dot(yf_ref[0], w_ref[0:nf, :], preferred_element_type=F32)
            + jnp.dot(ym_ref[0], w_ref[nf:, :], preferred_element_type=F32))
    y = ALPHA * x_ref[0] + (1.0 + gt_ref[0]) * hmix
    o_ref[0] = _ln(y) * lng_ref[...] + lnb_ref[...]


def _outproj(y_fox, y_mlstm, x, ada3, idx, w_out, ln_g, ln_b):
    b, s, d = x.shape
    tm = min(512, s)
    return pl.pallas_call(
        _outproj_kernel,
        grid=(b, s // tm),
        in_specs=[pl.BlockSpec((1, tm, FOX_WIDTH), lambda bi, i: (bi, i, 0)),
                  pl.BlockSpec((1, tm, MLSTM_WIDTH), lambda bi, i: (bi, i, 0)),
                  pl.BlockSpec((1, tm, d), lambda bi, i: (bi, i, 0)),
                  pl.BlockSpec((1, 1, d), lambda bi, i: (bi, 0, idx)),
                  pl.BlockSpec(w_out.shape, lambda bi, i: (0, 0)),
                  pl.BlockSpec((1, d), lambda bi, i: (0, 0)),
                  pl.BlockSpec((1, d), lambda bi, i: (0, 0))],
        out_specs=pl.BlockSpec((1, tm, d), lambda bi, i: (bi, i, 0)),
        out_shape=jax.ShapeDtypeStruct((b, s, d), F32),
        compiler_params=_cparams(("parallel", "parallel")),
        name="out_proj",
    )(y_fox, y_mlstm, x, ada3, w_out, ln_g, ln_b)


def _layer(x, c, w_ada, b_ada, ffn1_w_in, ffn1_w_out, ln1_g, ln1_b, w_in, fox_f_bias, mlstm_conv_w,
           mlstm_conv_b, mlstm_i_bias, mlstm_f_bias, mlstm_norm_g, w_out, ln2_g, ln2_b,
           ffn2_w_in, ffn2_w_out, ln3_g, ln3_b):
    b, s, d = x.shape
    row = lambda a: a.reshape(1, -1)

    c_pad = jnp.zeros((8, d), F32).at[:b].set(c)
    ada3 = _ada(c_pad, w_ada, row(b_ada)).reshape(8, 1, N_ADA * d)

    x = _ffn(x, ada3, 0, ffn1_w_in.astype(BF16), ffn1_w_out.astype(BF16), row(ln1_g), row(ln1_b))

    w_mix = jnp.concatenate([w_in[:, :COL_FOX_F], w_in[:, COL_MLSTM_Q:COL_MLSTM_I], w_in[:, COL_MLSTM_O:]],
                            axis=1).astype(BF16)
    w_gate = jnp.concatenate([w_in[:, COL_FOX_F:COL_MLSTM_Q], w_in[:, COL_MLSTM_I:COL_MLSTM_O],
                              jnp.zeros((d, GATE_LANES - G_ROWS), F32)], axis=1).astype(BF16)
    gate_bias = jnp.concatenate([fox_f_bias, mlstm_i_bias, mlstm_f_bias,
                                 jnp.zeros((GATE_LANES - G_ROWS,), F32)]).reshape(1, GATE_LANES)
    z, zg = _inproj(x, ada3, 3, w_mix, w_gate)
    gates_n, gates_t = _gates(zg, gate_bias)
    y_fox = _fox(z, gates_t)
    y_mlstm = _mlstm(z, gates_n, gates_t, mlstm_conv_w, row(mlstm_conv_b), row(mlstm_norm_g))
    x = _outproj(y_fox, y_mlstm, x, ada3, 5, w_out.astype(BF16), row(ln2_g), row(ln2_b))

    x = _ffn(x, ada3, 6, ffn2_w_in.astype(BF16), ffn2_w_out.astype(BF16), row(ln3_g), row(ln3_b))
    return x


def kernel(x, c, w_ada, b_ada, ffn1_w_in, ffn1_w_out, ln1_g, ln1_b, w_in, fox_f_bias, mlstm_conv_w,
           mlstm_conv_b, mlstm_i_bias, mlstm_f_bias, mlstm_norm_g, w_out, ln2_g, ln2_b,
           ffn2_w_in, ffn2_w_out, ln3_g, ln3_b):
    for l in range(DEPTH):
        x = _layer(x, c, w_ada[l], b_ada[l], ffn1_w_in[l], ffn1_w_out[l], ln1_g[l], ln1_b[l],
                   w_in[l], fox_f_bias[l], mlstm_conv_w[l], mlstm_conv_b[l], mlstm_i_bias[l],
                   mlstm_f_bias[l], mlstm_norm_g[l], w_out[l], ln2_g[l], ln2_b[l],
                   ffn2_w_in[l], ffn2_w_out[l], ln3_g[l], ln3_b[l])
    return x
```

```python
import functools

import jax
import jax.numpy as jnp
from jax import lax
from jax.experimental import pallas as pl
from jax.experimental.pallas import tpu as pltpu

F32 = jnp.float32
BF16 = jnp.bfloat16

D_MODEL = 2048
DEPTH = 1
N_FOX_HEADS = 8
FOX_HEAD_DIM = 128
FOX_WIDTH = N_FOX_HEADS * FOX_HEAD_DIM
N_MLSTM_HEADS = 4
MLSTM_V_DIM = 256
MLSTM_QK_DIM = 128
MLSTM_WIDTH = N_MLSTM_HEADS * MLSTM_V_DIM
MLSTM_QK_WIDTH = N_MLSTM_HEADS * MLSTM_QK_DIM
CONV_WIDTH = 4
D_FF = 5632
N_ADA = 9
ALPHA = (2 * DEPTH) ** 0.25
LN_EPS = 1e-5

COL_FOX_F = 3 * FOX_WIDTH
COL_MLSTM_Q = COL_FOX_F + N_FOX_HEADS
COL_MLSTM_I = COL_MLSTM_Q + 2 * MLSTM_QK_WIDTH + MLSTM_WIDTH
COL_MLSTM_O = COL_MLSTM_I + 2 * N_MLSTM_HEADS
IN_WIDTH = COL_MLSTM_O + MLSTM_WIDTH

GATE_LANES = 128
G_FOX = 0
G_MI = N_FOX_HEADS
G_MF = G_MI + N_MLSTM_HEADS
G_ROWS = 16
MIX_COLS = 3 * FOX_WIDTH + 2 * MLSTM_QK_WIDTH + 2 * MLSTM_WIDTH

MLSTM_CHUNK = 256
NEG_BIG = -1e30
LOG2E = 1.4426950408889634
FOX_AUG = 3
VMEM_LIMIT = 56 * 1024 * 1024


def _cparams(sem):
    return pltpu.CompilerParams(dimension_semantics=sem, vmem_limit_bytes=VMEM_LIMIT)


def _ln(x):
    mu = jnp.mean(x, axis=-1, keepdims=True)
    xc = x - mu
    var = jnp.mean(xc * xc, axis=-1, keepdims=True)
    return xc * lax.rsqrt(var + LN_EPS)


def _sigmoid(x):
    return 1.0 / (1.0 + jnp.exp(-x))


def _log_sigmoid(x):
    return jnp.minimum(x, 0.0) - jnp.log1p(jnp.exp(-jnp.abs(x)))


def _ada_kernel(c_ref, w_ref, b_ref, o_ref):
    c = c_ref[...]
    s = (c * _sigmoid(c)).astype(BF16)
    o_ref[...] = jnp.dot(s, w_ref[...].astype(BF16), preferred_element_type=F32) + b_ref[...]


def _ada(c_pad, w_ada, b_ada):
    rows, d = c_pad.shape
    n = w_ada.shape[1]
    tn = 1024
    return pl.pallas_call(
        _ada_kernel,
        grid=(n // tn,),
        in_specs=[pl.BlockSpec((rows, d), lambda j: (0, 0)),
                  pl.BlockSpec((d, tn), lambda j: (0, j)),
                  pl.BlockSpec((1, tn), lambda j: (0, j))],
        out_specs=pl.BlockSpec((rows, tn), lambda j: (0, j)),
        out_shape=jax.ShapeDtypeStruct((rows, n), F32),
        compiler_params=_cparams(("arbitrary",)),
        name="ada_proj",
    )(c_pad, w_ada, b_ada)


def _ffn_kernel(x_ref, sh_ref, sc_ref, gt_ref, wg_ref, wu_ref, wo_ref, lng_ref, lnb_ref, o_ref, h_ref):
    k = pl.program_id(2)

    @pl.when(k == 0)
    def _():
        h = _ln(x_ref[0]) * (1.0 + sc_ref[0]) + sh_ref[0]
        h_ref[...] = h.astype(BF16)

    h = h_ref[...]
    g = jnp.dot(h, wg_ref[...], preferred_element_type=F32)
    u = jnp.dot(h, wu_ref[...], preferred_element_type=F32)
    a = (g * _sigmoid(g) * u).astype(BF16)
    part = jnp.dot(a, wo_ref[...], preferred_element_type=F32)

    @pl.when(k == 0)
    def _():
        o_ref[0] = part

    @pl.when(k > 0)
    def _():
        o_ref[0] += part

    @pl.when(k == pl.num_programs(2) - 1)
    def _():
        y = ALPHA * x_ref[0] + (0.5 * (1.0 + gt_ref[0])) * o_ref[0]
        o_ref[0] = _ln(y) * lng_ref[...] + lnb_ref[...]


def _ffn(x, ada3, idx, w_in, w_out, ln_g, ln_b):
    b, s, d = x.shape
    dff = w_out.shape[0]
    tm = min(512, s)
    tf = 512
    nk = dff // tf
    vec = lambda j: pl.BlockSpec((1, 1, d), lambda bi, i, k: (bi, 0, j))
    return pl.pallas_call(
        _ffn_kernel,
        grid=(b, s // tm, nk),
        in_specs=[pl.BlockSpec((1, tm, d), lambda bi, i, k: (bi, i, 0)),
                  vec(idx), vec(idx + 1), vec(idx + 2),
                  pl.BlockSpec((d, tf), lambda bi, i, k: (0, k)),
                  pl.BlockSpec((d, tf), lambda bi, i, k: (0, k + nk)),
                  pl.BlockSpec((tf, d), lambda bi, i, k: (k, 0)),
                  pl.BlockSpec((1, d), lambda bi, i, k: (0, 0)),
                  pl.BlockSpec((1, d), lambda bi, i, k: (0, 0))],
        out_specs=pl.BlockSpec((1, tm, d), lambda bi, i, k: (bi, i, 0)),
        out_shape=jax.ShapeDtypeStruct((b, s, d), F32),
        scratch_shapes=[pltpu.VMEM((tm, d), BF16)],
        compiler_params=_cparams(("parallel", "parallel", "arbitrary")),
        name="ffn",
    )(x, ada3, ada3, ada3, w_in, w_in, w_out, ln_g, ln_b)


def _inproj_kernel(x_ref, sh_ref, sc_ref, w_ref, wgate_ref, z_ref, zg_ref, h_ref, *, n_scaled, q_scale):
    j = pl.program_id(2)

    @pl.when(j == 0)
    def _():
        h = (_ln(x_ref[0]) * (1.0 + sc_ref[0]) + sh_ref[0]).astype(BF16)
        h_ref[...] = h
        zg_ref[0] = jnp.dot(h, wgate_ref[...], preferred_element_type=F32)

    z = jnp.dot(h_ref[...], w_ref[...], preferred_element_type=F32)
    z = z * jnp.where(j < n_scaled, q_scale, 1.0).astype(F32)
    z_ref[0] = z.astype(BF16)


def _inproj(x, ada3, idx, w_mix, w_gate):
    b, s, d = x.shape
    n = w_mix.shape[1]
    tm = min(512, s)
    tn = 512
    vec = lambda j: pl.BlockSpec((1, 1, d), lambda bi, i, jj: (bi, 0, j))
    kern = functools.partial(_inproj_kernel, n_scaled=FOX_WIDTH // tn, q_scale=FOX_HEAD_DIM ** -0.5 * LOG2E)
    return pl.pallas_call(
        kern,
        grid=(b, s // tm, n // tn),
        in_specs=[pl.BlockSpec((1, tm, d), lambda bi, i, j: (bi, i, 0)),
                  vec(idx), vec(idx + 1),
                  pl.BlockSpec((d, tn), lambda bi, i, j: (0, j)),
                  pl.BlockSpec((d, GATE_LANES), lambda bi, i, j: (0, 0))],
        out_specs=[pl.BlockSpec((1, tm, tn), lambda bi, i, j: (bi, i, j)),
                   pl.BlockSpec((1, tm, GATE_LANES), lambda bi, i, j: (bi, i, 0))],
        out_shape=[jax.ShapeDtypeStruct((b, s, n), BF16),
                   jax.ShapeDtypeStruct((b, s, GATE_LANES), F32)],
        scratch_shapes=[pltpu.VMEM((tm, d), BF16)],
        compiler_params=_cparams(("parallel", "parallel", "arbitrary")),
        name="in_proj",
    )(x, ada3, ada3, w_mix, w_gate)


def _split3(v):
    hi = v.astype(BF16)
    r = v - hi.astype(F32)
    mid = r.astype(BF16)
    lo = (r - mid.astype(F32)).astype(BF16)
    return hi, mid, lo


def _gate_kernel(zg_ref, bias_ref, nat_ref, t_ref, kx_ref, carry_ref):
    i = pl.program_id(1)

    @pl.when(i == 0)
    def _():
        carry_ref[...] = jnp.zeros_like(carry_ref)

    z = zg_ref[0] + bias_ref[...]
    rows = z.shape[0]
    col = lax.broadcasted_iota(jnp.int32, z.shape, 1)
    is_in_gate = (col >= G_MI) & (col < G_MF)
    v = jnp.where(is_in_gate, z, _log_sigmoid(z))
    r_i = lax.broadcasted_iota(jnp.int32, (rows, rows), 0)
    c_i = lax.broadcasted_iota(jnp.int32, (rows, rows), 1)
    tri = jnp.where(r_i >= c_i, 1.0, 0.0).astype(BF16)
    hi, mid, lo = _split3(v)
    csum = (jnp.dot(tri, hi, preferred_element_type=F32)
            + jnp.dot(tri, mid, preferred_element_type=F32)
            + jnp.dot(tri, lo, preferred_element_type=F32))
    running = csum + carry_ref[...]
    carry_ref[...] = running[rows - 1:rows, :]
    out = jnp.where(col < G_MI, running, jnp.where(is_in_gate, v, csum))
    nat_ref[0] = out
    t_ref[0] = out.T[:G_ROWS, :]
    for h in range(N_FOX_HEADS):
        parts = _split3(jnp.broadcast_to(running[:, h:h + 1] * (-LOG2E), z.shape))
        tile = jnp.zeros(z.shape, F32)
        for t in reversed(range(FOX_AUG)):
            tile = jnp.where(col == t, parts[t].astype(F32), tile)
        kx_ref[0, :, h * GATE_LANES:(h + 1) * GATE_LANES] = tile.astype(BF16)


def _gates(zg, bias):
    b, s, _ = zg.shape
    tl = min(MLSTM_CHUNK, s)
    return pl.pallas_call(
        _gate_kernel,
        grid=(b, s // tl),
        in_specs=[pl.BlockSpec((1, tl, GATE_LANES), lambda bi, i: (bi, i, 0)),
                  pl.BlockSpec((1, GATE_LANES), lambda bi, i: (0, 0))],
        out_specs=[pl.BlockSpec((1, tl, GATE_LANES), lambda bi, i: (bi, i, 0)),
                   pl.BlockSpec((1, G_ROWS, tl), lambda bi, i: (bi, 0, i)),
                   pl.BlockSpec((1, tl, N_FOX_HEADS * GATE_LANES), lambda bi, i: (bi, i, 0))],
        out_shape=[jax.ShapeDtypeStruct((b, s, GATE_LANES), F32),
                   jax.ShapeDtypeStruct((b, G_ROWS, s), F32),
                   jax.ShapeDtypeStruct((b, s, N_FOX_HEADS * GATE_LANES), BF16)],
        scratch_shapes=[pltpu.VMEM((1, GATE_LANES), F32)],
        compiler_params=_cparams(("parallel", "arbitrary")),
        name="gates",
    )(zg, bias)


def _fox_kernel(q_ref, k_ref, kx_ref, v_ref, o_ref, s_buf, p_buf, a_buf, m_ref, acc_ref, *, tq, tk):
    qi = pl.program_id(2)
    dh = FOX_HEAD_DIM
    lane_q = lax.broadcasted_iota(jnp.int32, (tq, dh), 1)
    lane_k = lax.broadcasted_iota(jnp.int32, (tk, dh), 1)
    q = jnp.concatenate([q_ref[0], jnp.where(lane_q < FOX_AUG, 1.0, 0.0).astype(BF16)], axis=1)
    v_ones = jnp.where(lane_k == 0, 1.0, 0.0).astype(BF16)

    def logits(j):
        start = pl.multiple_of(j * tk, tk)
        k = jnp.concatenate([k_ref[0, pl.ds(start, tk), :], kx_ref[0, pl.ds(start, tk), :]], axis=1)
        return lax.dot_general(q, k, (((1,), (1,)), ((), ())), preferred_element_type=F32)

    def softmax(slot, masked):
        s = s_buf[slot]
        if masked:
            r_i = lax.broadcasted_iota(jnp.int32, s.shape, 0)
            c_i = lax.broadcasted_iota(jnp.int32, s.shape, 1)
            s = jnp.where(c_i <= r_i, s, NEG_BIG)
        m = m_ref[...]
        m_new = jnp.maximum(m, jnp.broadcast_to(jnp.max(s, axis=-1, keepdims=True), m.shape))
        for c in range(tk // dh):
            p_buf[slot, :, c * dh:(c + 1) * dh] = jnp.exp2(s[:, c * dh:(c + 1) * dh] - m_new).astype(BF16)
        a_buf[slot] = jnp.exp2(m - m_new)
        m_ref[...] = m_new

    def accumulate(slot, j):
        start = pl.multiple_of(j * tk, tk)
        v = jnp.concatenate([v_ref[0, pl.ds(start, tk), :], v_ones], axis=1)
        pv = jnp.dot(p_buf[slot], v, preferred_element_type=F32)
        a = a_buf[slot]
        for c in range(2):
            acc_ref[:, c * dh:(c + 1) * dh] = a * acc_ref[:, c * dh:(c + 1) * dh] + pv[:, c * dh:(c + 1) * dh]

    n = qi
    m_ref[...] = jnp.full(m_ref.shape, NEG_BIG, F32)
    acc_ref[...] = jnp.zeros(acc_ref.shape, F32)
    s_buf[0] = logits(qi)
    s_buf[1] = logits(0)
    softmax(0, True)

    def pair(i, carry):
        t = 2 * i + 1
        s_buf[0] = logits(t)
        softmax(1, False)
        accumulate(0, jnp.where(i == 0, qi, t - 2))
        s_buf[1] = logits(t + 1)
        softmax(0, False)
        accumulate(1, t - 1)
        return carry

    lax.fori_loop(0, n // 2, pair, 0)

    @pl.when(n % 2 == 1)
    def _():
        softmax(1, False)
        accumulate(0, jnp.where(n == 1, qi, n - 2))
        accumulate(1, n - 1)

    @pl.when(n % 2 == 0)
    def _():
        accumulate(0, jnp.where(n == 0, qi, n - 1))

    acc = acc_ref[...]
    o_ref[0] = (acc[:, :dh] / acc[:, dh:dh + 1]).astype(o_ref.dtype)


def _fox(z, kx):
    b, s, _ = z.shape
    tq = min(512, s)
    tk = min(512, s)
    h = N_FOX_HEADS
    return pl.pallas_call(
        functools.partial(_fox_kernel, tq=tq, tk=tk),
        grid=(b, h, s // tq),
        in_specs=[pl.BlockSpec((1, tq, FOX_HEAD_DIM), lambda bi, hi, qi: (bi, qi, hi)),
                  pl.BlockSpec((1, s, FOX_HEAD_DIM), lambda bi, hi, qi: (bi, 0, h + hi)),
                  pl.BlockSpec((1, s, GATE_LANES), lambda bi, hi, qi: (bi, 0, hi)),
                  pl.BlockSpec((1, s, FOX_HEAD_DIM), lambda bi, hi, qi: (bi, 0, 2 * h + hi))],
        out_specs=pl.BlockSpec((1, tq, FOX_HEAD_DIM), lambda bi, hi, qi: (bi, qi, hi)),
        out_shape=jax.ShapeDtypeStruct((b, s, FOX_WIDTH), BF16),
        scratch_shapes=[pltpu.VMEM((2, tq, tk), F32),
                        pltpu.VMEM((2, tq, tk), BF16),
                        pltpu.VMEM((2, tq, FOX_HEAD_DIM), F32),
                        pltpu.VMEM((tq, FOX_HEAD_DIM), F32),
                        pltpu.VMEM((tq, 2 * FOX_HEAD_DIM), F32)],
        compiler_params=_cparams(("parallel", "parallel", "arbitrary")),
        name="fox_attn",
    )(z, z, kx, z)


def _mlstm_kernel(qk_ref, v_ref, o_ref, gn_ref, gt_ref, cw_ref, cb_ref, ng_ref, y_ref,
                  c_st, n_st, m_st, tail, ubuf):
    c = pl.program_id(1)
    L = qk_ref.shape[1]
    dk, dv = MLSTM_QK_DIM, MLSTM_V_DIM

    @pl.when(c == 0)
    def _():
        c_st[...] = jnp.zeros_like(c_st)
        n_st[...] = jnp.zeros_like(n_st)
        m_st[...] = jnp.zeros_like(m_st)
        tail[...] = jnp.zeros_like(tail)

    u = qk_ref[0].astype(F32)
    ubuf[0:8, :] = tail[...]
    ubuf[8:8 + L, :] = u
    tail[...] = u[L - 8:L, :]
    conv = cb_ref[...] + cw_ref[3:4, :] * u
    for d in range(1, CONV_WIDTH):
        conv = conv + cw_ref[3 - d:4 - d, :] * ubuf[8 - d:8 - d + L, :]
    qk = conv * _sigmoid(conv)

    gn = gn_ref[0]
    gt = gt_ref[0]
    r_i = lax.broadcasted_iota(jnp.int32, (L, L), 0)
    c_i = lax.broadcasted_iota(jnp.int32, (L, L), 1)
    causal = c_i <= r_i

    for h in range(N_MLSTM_HEADS):
        qf = qk[:, h * dk:(h + 1) * dk]
        q = qf.astype(BF16)
        kf = qk[:, MLSTM_QK_WIDTH + h * dk:MLSTM_QK_WIDTH + (h + 1) * dk] * (dk ** -0.5)
        v = v_ref[0, :, h * dv:(h + 1) * dv]
        bcol = gn[:, G_MF + h:G_MF + h + 1]
        icol = gn[:, G_MI + h:G_MI + h + 1]
        brow = gt[G_MF + h:G_MF + h + 1, :]
        irow = gt[G_MI + h:G_MI + h + 1, :]
        m_prev = m_st[h][:, 0:1]
        c_prev = c_st[h]
        n_prev = n_st[h]

        m_inter = bcol + m_prev
        dlog = jnp.where(causal, bcol - brow + irow, NEG_BIG)
        m_t = jnp.maximum(m_inter, jnp.max(dlog, axis=-1, keepdims=True))
        s = lax.dot_general(q, kf.astype(BF16), (((1,), (1,)), ((), ())), preferred_element_type=F32)
        s = s * jnp.exp(dlog - m_t)
        inter = jnp.exp(m_inter - m_t)
        num = (inter * jnp.dot(q, c_prev.astype(BF16), preferred_element_type=F32)
               + jnp.dot(s.astype(BF16), v, preferred_element_type=F32))
        den = inter * jnp.sum(qf * n_prev, axis=-1, keepdims=True) + jnp.sum(s, axis=-1, keepdims=True)
        hh = num / jnp.maximum(jnp.abs(den), jnp.exp(-m_t))

        b_last = bcol[L - 1:L, :]
        wlog = b_last - bcol + icol
        m_new = jnp.maximum(b_last + m_prev, jnp.max(wlog, axis=0, keepdims=True))
        decay = jnp.exp(b_last + m_prev - m_new)
        wk = kf * jnp.exp(wlog - m_new)
        c_st[h] = decay * c_prev + jnp.dot(wk.T.astype(BF16), v, preferred_element_type=F32)
        n_st[h] = decay * n_prev + jnp.sum(wk, axis=0, keepdims=True)
        m_st[h] = jnp.broadcast_to(m_new, (1, GATE_LANES))

        hn = hh * lax.rsqrt(jnp.mean(hh * hh, axis=-1, keepdims=True) + LN_EPS)
        hn = hn * ng_ref[:, h * dv:(h + 1) * dv]
        og = _sigmoid(o_ref[0, :, h * dv:(h + 1) * dv].astype(F32))
        y_ref[0, :, h * dv:(h + 1) * dv] = (og * hn).astype(y_ref.dtype)


def _mlstm(z, gates_n, gates_t, conv_w, conv_b, norm_g):
    b, s, _ = z.shape
    L = min(MLSTM_CHUNK, s)
    wq = 2 * MLSTM_QK_WIDTH
    assert wq == MLSTM_WIDTH == FOX_WIDTH
    base = 3 * FOX_WIDTH // wq
    return pl.pallas_call(
        _mlstm_kernel,
        grid=(b, s // L),
        in_specs=[pl.BlockSpec((1, L, wq), lambda bi, ci: (bi, ci, base)),
                  pl.BlockSpec((1, L, MLSTM_WIDTH), lambda bi, ci: (bi, ci, base + 1)),
                  pl.BlockSpec((1, L, MLSTM_WIDTH), lambda bi, ci: (bi, ci, base + 2)),
                  pl.BlockSpec((1, L, GATE_LANES), lambda bi, ci: (bi, ci, 0)),
                  pl.BlockSpec((1, G_ROWS, L), lambda bi, ci: (bi, 0, ci)),
                  pl.BlockSpec((CONV_WIDTH, wq), lambda bi, ci: (0, 0)),
                  pl.BlockSpec((1, wq), lambda bi, ci: (0, 0)),
                  pl.BlockSpec((1, MLSTM_WIDTH), lambda bi, ci: (0, 0))],
        out_specs=pl.BlockSpec((1, L, MLSTM_WIDTH), lambda bi, ci: (bi, ci, 0)),
        out_shape=jax.ShapeDtypeStruct((b, s, MLSTM_WIDTH), BF16),
        scratch_shapes=[pltpu.VMEM((N_MLSTM_HEADS, MLSTM_QK_DIM, MLSTM_V_DIM), F32),
                        pltpu.VMEM((N_MLSTM_HEADS, 1, MLSTM_QK_DIM), F32),
                        pltpu.VMEM((N_MLSTM_HEADS, 1, GATE_LANES), F32),
                        pltpu.VMEM((8, wq), F32),
                        pltpu.VMEM((L + 8, wq), F32)],
        compiler_params=_cparams(("parallel", "arbitrary")),
        name="mlstm",
    )(z, z, z, gates_n, gates_t, conv_w, conv_b, norm_g)


def _outproj_kernel(yf_ref, ym_ref, x_ref, gt_ref, w_ref, lng_ref, lnb_ref, o_ref):
    nf = yf_ref.shape[2]
    hmix = (jnp.dot(yf_ref[0], w_ref[0:nf, :], preferred_element_type=F32)
            + jnp.dot(ym_ref[0], w_ref[nf:, :], preferred_element_type=F32))
    y = ALPHA * x_ref[0] + (1.0 + gt_ref[0]) * hmix
    o_ref[0] = _ln(y) * lng_ref[...] + lnb_ref[...]


def _outproj(y_fox, y_mlstm, x, ada3, idx, w_out, ln_g, ln_b):
    b, s, d = x.shape
    tm = min(512, s)
    return pl.pallas_call(
        _outproj_kernel,
        grid=(b, s // tm),
        in_specs=[pl.BlockSpec((1, tm, FOX_WIDTH), lambda bi, i: (bi, i, 0)),
                  pl.BlockSpec((1, tm, MLSTM_WIDTH), lambda bi, i: (bi, i, 0)),
                  pl.BlockSpec((1, tm, d), lambda bi, i: (bi, i, 0)),
                  pl.BlockSpec((1, 1, d), lambda bi, i: (bi, 0, idx)),
                  pl.BlockSpec(w_out.shape, lambda bi, i: (0, 0)),
                  pl.BlockSpec((1, d), lambda bi, i: (0, 0)),
                  pl.BlockSpec((1, d), lambda bi, i: (0, 0))],
        out_specs=pl.BlockSpec((1, tm, d), lambda bi, i: (bi, i, 0)),
        out_shape=jax.ShapeDtypeStruct((b, s, d), F32),
        compiler_params=_cparams(("parallel", "parallel")),
        name="out_proj",
    )(y_fox, y_mlstm, x, ada3, w_out, ln_g, ln_b)


def _layer(x, c, w_ada, b_ada, ffn1_w_in, ffn1_w_out, ln1_g, ln1_b, w_in, fox_f_bias, mlstm_conv_w,
           mlstm_conv_b, mlstm_i_bias, mlstm_f_bias, mlstm_norm_g, w_out, ln2_g, ln2_b,
           ffn2_w_in, ffn2_w_out, ln3_g, ln3_b):
    b, s, d = x.shape
    row = lambda a: a.reshape(1, -1)

    c_pad = jnp.zeros((8, d), F32).at[:b].set(c)
    ada3 = _ada(c_pad, w_ada, row(b_ada)).reshape(8, 1, N_ADA * d)

    x = _ffn(x, ada3, 0, ffn1_w_in.astype(BF16), ffn1_w_out.astype(BF16), row(ln1_g), row(ln1_b))

    w_mix = jnp.concatenate([w_in[:, :COL_FOX_F], w_in[:, COL_MLSTM_Q:COL_MLSTM_I], w_in[:, COL_MLSTM_O:]],
                            axis=1).astype(BF16)
    w_gate = jnp.concatenate([w_in[:, COL_FOX_F:COL_MLSTM_Q], w_in[:, COL_MLSTM_I:COL_MLSTM_O],
                              jnp.zeros((d, GATE_LANES - G_ROWS), F32)], axis=1).astype(BF16)
    gate_bias = jnp.concatenate([fox_f_bias, mlstm_i_bias, mlstm_f_bias,
                                 jnp.zeros((GATE_LANES - G_ROWS,), F32)]).reshape(1, GATE_LANES)
    z, zg = _inproj(x, ada3, 3, w_mix, w_gate)
    gates_n, gates_t, kx = _gates(zg, gate_bias)
    y_fox = _fox(z, kx)
    y_mlstm = _mlstm(z, gates_n, gates_t, mlstm_conv_w, row(mlstm_conv_b), row(mlstm_norm_g))
    x = _outproj(y_fox, y_mlstm, x, ada3, 5, w_out.astype(BF16), row(ln2_g), row(ln2_b))

    x = _ffn(x, ada3, 6, ffn2_w_in.astype(BF16), ffn2_w_out.astype(BF16), row(ln3_g), row(ln3_b))
    return x


def kernel(x, c, w_ada, b_ada, ffn1_w_in, ffn1_w_out, ln1_g, ln1_b, w_in, fox_f_bias, mlstm_conv_w,
           mlstm_conv_b, mlstm_i_bias, mlstm_f_bias, mlstm_norm_g, w_out, ln2_g, ln2_b,
           ffn2_w_in, ffn2_w_out, ln3_g, ln3_b):
    for l in range(DEPTH):
        x = _layer(x, c, w_ada[l], b_ada[l], ffn1_w_in[l], ffn1_w_out[l], ln1_g[l], ln1_b[l],
                   w_in[l], fox_f_bias[l], mlstm_conv_w[l], mlstm_conv_b[l], mlstm_i_bias[l],
                   mlstm_f_bias[l], mlstm_norm_g[l], w_out[l], ln2_g[l], ln2_b[l],
                   ffn2_w_in[l], ffn2_w_out[l], ln3_g[l], ln3_b[l])
    return x
```

```python
import functools

import jax
import jax.numpy as jnp
from jax import lax
from jax.experimental import pallas as pl
from jax.experimental.pallas import tpu as pltpu

F32 = jnp.float32
BF16 = jnp.bfloat16

D_MODEL = 2048
DEPTH = 1
N_FOX_HEADS = 8
FOX_HEAD_DIM = 128
FOX_WIDTH = N_FOX_HEADS * FOX_HEAD_DIM
N_MLSTM_HEADS = 4
MLSTM_V_DIM = 256
MLSTM_QK_DIM = 128
MLSTM_WIDTH = N_MLSTM_HEADS * MLSTM_V_DIM
MLSTM_QK_WIDTH = N_MLSTM_HEADS * MLSTM_QK_DIM
CONV_WIDTH = 4
D_FF = 5632
N_ADA = 9
ALPHA = (2 * DEPTH) ** 0.25
LN_EPS = 1e-5

COL_FOX_F = 3 * FOX_WIDTH
COL_MLSTM_Q = COL_FOX_F + N_FOX_HEADS
COL_MLSTM_I = COL_MLSTM_Q + 2 * MLSTM_QK_WIDTH + MLSTM_WIDTH
COL_MLSTM_O = COL_MLSTM_I + 2 * N_MLSTM_HEADS
IN_WIDTH = COL_MLSTM_O + MLSTM_WIDTH

GATE_LANES = 128
G_FOX = 0
G_MI = N_FOX_HEADS
G_MF = G_MI + N_MLSTM_HEADS
G_ROWS = 16
MIX_COLS = 3 * FOX_WIDTH + 2 * MLSTM_QK_WIDTH + 2 * MLSTM_WIDTH

MLSTM_CHUNK = 256
NEG_BIG = -1e30
LOG2E = 1.4426950408889634
FOX_AUG = 3
VMEM_LIMIT = 56 * 1024 * 1024


def _cparams(sem):
    return pltpu.CompilerParams(dimension_semantics=sem, vmem_limit_bytes=VMEM_LIMIT)


def _ln(x):
    mu = jnp.mean(x, axis=-1, keepdims=True)
    xc = x - mu
    var = jnp.mean(xc * xc, axis=-1, keepdims=True)
    return xc * lax.rsqrt(var + LN_EPS)


def _sigmoid(x):
    return 1.0 / (1.0 + jnp.exp(-x))


def _log_sigmoid(x):
    return jnp.minimum(x, 0.0) - jnp.log1p(jnp.exp(-jnp.abs(x)))


def _ada_kernel(c_ref, w_ref, b_ref, o_ref):
    c = c_ref[...]
    s = (c * _sigmoid(c)).astype(BF16)
    o_ref[...] = jnp.dot(s, w_ref[...].astype(BF16), preferred_element_type=F32) + b_ref[...]


def _ada(c_pad, w_ada, b_ada):
    rows, d = c_pad.shape
    n = w_ada.shape[1]
    tn = 1024
    return pl.pallas_call(
        _ada_kernel,
        grid=(n // tn,),
        in_specs=[pl.BlockSpec((rows, d), lambda j: (0, 0)),
                  pl.BlockSpec((d, tn), lambda j: (0, j)),
                  pl.BlockSpec((1, tn), lambda j: (0, j))],
        out_specs=pl.BlockSpec((rows, tn), lambda j: (0, j)),
        out_shape=jax.ShapeDtypeStruct((rows, n), F32),
        compiler_params=_cparams(("arbitrary",)),
        name="ada_proj",
    )(c_pad, w_ada, b_ada)


def _ffn_kernel(x_ref, sh_ref, sc_ref, gt_ref, wg_ref, wu_ref, wo_ref, lng_ref, lnb_ref, o_ref, h_ref):
    k = pl.program_id(2)
    last = pl.num_programs(2) - 1

    def swiglu_part(h):
        g = jnp.dot(h, wg_ref[...], preferred_element_type=F32)
        u = jnp.dot(h, wu_ref[...], preferred_element_type=F32)
        a = (g * _sigmoid(g) * u).astype(BF16)
        return jnp.dot(a, wo_ref[...], preferred_element_type=F32)

    @pl.when(k == 0)
    def _():
        h = (_ln(x_ref[0]) * (1.0 + sc_ref[0]) + sh_ref[0]).astype(BF16)
        h_ref[...] = h
        o_ref[0] = swiglu_part(h)

    @pl.when((k > 0) & (k < last))
    def _():
        o_ref[0] += swiglu_part(h_ref[...])

    @pl.when(k == last)
    def _():
        y = ALPHA * x_ref[0] + (0.5 * (1.0 + gt_ref[0])) * (o_ref[0] + swiglu_part(h_ref[...]))
        o_ref[0] = _ln(y) * lng_ref[...] + lnb_ref[...]


def _ffn(x, ada3, idx, w_in, w_out, ln_g, ln_b):
    b, s, d = x.shape
    dff = w_out.shape[0]
    tm = min(512, s)
    tf = 512
    nk = dff // tf
    vec = lambda j: pl.BlockSpec((1, 1, d), lambda bi, i, k: (bi, 0, j))
    return pl.pallas_call(
        _ffn_kernel,
        grid=(b, s // tm, nk),
        in_specs=[pl.BlockSpec((1, tm, d), lambda bi, i, k: (bi, i, 0)),
                  vec(idx), vec(idx + 1), vec(idx + 2),
                  pl.BlockSpec((d, tf), lambda bi, i, k: (0, k)),
                  pl.BlockSpec((d, tf), lambda bi, i, k: (0, k + nk)),
                  pl.BlockSpec((tf, d), lambda bi, i, k: (k, 0)),
                  pl.BlockSpec((1, d), lambda bi, i, k: (0, 0)),
                  pl.BlockSpec((1, d), lambda bi, i, k: (0, 0))],
        out_specs=pl.BlockSpec((1, tm, d), lambda bi, i, k: (bi, i, 0)),
        out_shape=jax.ShapeDtypeStruct((b, s, d), F32),
        scratch_shapes=[pltpu.VMEM((tm, d), BF16)],
        compiler_params=_cparams(("parallel", "parallel", "arbitrary")),
        name="ffn",
    )(x, ada3, ada3, ada3, w_in, w_in, w_out, ln_g, ln_b)


def _inproj_kernel(x_ref, sh_ref, sc_ref, w_ref, wgate_ref, z_ref, zg_ref, h_ref, *, n_scaled, q_scale):
    j = pl.program_id(2)

    def project(h):
        z = jnp.dot(h, w_ref[...], preferred_element_type=F32)
        z = z * jnp.where(j < n_scaled, q_scale, 1.0).astype(F32)
        z_ref[0] = z.astype(BF16)

    @pl.when(j == 0)
    def _():
        h = (_ln(x_ref[0]) * (1.0 + sc_ref[0]) + sh_ref[0]).astype(BF16)
        h_ref[...] = h
        zg_ref[0] = jnp.dot(h, wgate_ref[...], preferred_element_type=F32)
        project(h)

    @pl.when(j > 0)
    def _():
        project(h_ref[...])


def _inproj(x, ada3, idx, w_mix, w_gate):
    b, s, d = x.shape
    n = w_mix.shape[1]
    tm = min(1024, s)
    tn = 512
    vec = lambda j: pl.BlockSpec((1, 1, d), lambda bi, i, jj: (bi, 0, j))
    kern = functools.partial(_inproj_kernel, n_scaled=FOX_WIDTH // tn, q_scale=FOX_HEAD_DIM ** -0.5 * LOG2E)
    return pl.pallas_call(
        kern,
        grid=(b, s // tm, n // tn),
        in_specs=[pl.BlockSpec((1, tm, d), lambda bi, i, j: (bi, i, 0)),
                  vec(idx), vec(idx + 1),
                  pl.BlockSpec((d, tn), lambda bi, i, j: (0, j)),
                  pl.BlockSpec((d, GATE_LANES), lambda bi, i, j: (0, 0))],
        out_specs=[pl.BlockSpec((1, tm, tn), lambda bi, i, j: (bi, i, j)),
                   pl.BlockSpec((1, tm, GATE_LANES), lambda bi, i, j: (bi, i, 0))],
        out_shape=[jax.ShapeDtypeStruct((b, s, n), BF16),
                   jax.ShapeDtypeStruct((b, s, GATE_LANES), F32)],
        scratch_shapes=[pltpu.VMEM((tm, d), BF16)],
        compiler_params=_cparams(("parallel", "parallel", "arbitrary")),
        name="in_proj",
    )(x, ada3, ada3, w_mix, w_gate)


def _split3(v):
    hi = v.astype(BF16)
    r = v - hi.astype(F32)
    mid = r.astype(BF16)
    lo = (r - mid.astype(F32)).astype(BF16)
    return hi, mid, lo


def _gate_kernel(zg_ref, bias_ref, nat_ref, t_ref, kx_ref, carry_ref):
    i = pl.program_id(1)

    @pl.when(i == 0)
    def _():
        carry_ref[...] = jnp.zeros_like(carry_ref)

    z = zg_ref[0] + bias_ref[...]
    rows = z.shape[0]
    col = lax.broadcasted_iota(jnp.int32, z.shape, 1)
    is_in_gate = (col >= G_MI) & (col < G_MF)
    v = jnp.where(is_in_gate, z, _log_sigmoid(z))
    r_i = lax.broadcasted_iota(jnp.int32, (rows, rows), 0)
    c_i = lax.broadcasted_iota(jnp.int32, (rows, rows), 1)
    tri = jnp.where(r_i >= c_i, 1.0, 0.0).astype(BF16)
    hi, mid, lo = _split3(v)
    csum = (jnp.dot(tri, hi, preferred_element_type=F32)
            + jnp.dot(tri, mid, preferred_element_type=F32)
            + jnp.dot(tri, lo, preferred_element_type=F32))
    running = csum + carry_ref[...]
    carry_ref[...] = running[rows - 1:rows, :]
    out = jnp.where(col < G_MI, running, jnp.where(is_in_gate, v, csum))
    nat_ref[0] = out
    t_ref[0] = out.T[:G_ROWS, :]
    for h in range(N_FOX_HEADS):
        parts = _split3(jnp.broadcast_to(running[:, h:h + 1] * (-LOG2E), z.shape))
        tile = jnp.zeros(z.shape, F32)
        for t in reversed(range(FOX_AUG)):
            tile = jnp.where(col == t, parts[t].astype(F32), tile)
        kx_ref[0, :, h * GATE_LANES:(h + 1) * GATE_LANES] = tile.astype(BF16)


def _gates(zg, bias):
    b, s, _ = zg.shape
    tl = min(MLSTM_CHUNK, s)
    return pl.pallas_call(
        _gate_kernel,
        grid=(b, s // tl),
        in_specs=[pl.BlockSpec((1, tl, GATE_LANES), lambda bi, i: (bi, i, 0)),
                  pl.BlockSpec((1, GATE_LANES), lambda bi, i: (0, 0))],
        out_specs=[pl.BlockSpec((1, tl, GATE_LANES), lambda bi, i: (bi, i, 0)),
                   pl.BlockSpec((1, G_ROWS, tl), lambda bi, i: (bi, 0, i)),
                   pl.BlockSpec((1, tl, N_FOX_HEADS * GATE_LANES), lambda bi, i: (bi, i, 0))],
        out_shape=[jax.ShapeDtypeStruct((b, s, GATE_LANES), F32),
                   jax.ShapeDtypeStruct((b, G_ROWS, s), F32),
                   jax.ShapeDtypeStruct((b, s, N_FOX_HEADS * GATE_LANES), BF16)],
        scratch_shapes=[pltpu.VMEM((1, GATE_LANES), F32)],
        compiler_params=_cparams(("parallel", "arbitrary")),
        name="gates",
    )(zg, bias)


def _fox_kernel(q_ref, k_ref, kx_ref, v_ref, o_ref, s_buf, p_buf, a_buf, m_ref, acc_ref, *, tq, tk):
    qi = pl.program_id(2)
    dh = FOX_HEAD_DIM
    lane_q = lax.broadcasted_iota(jnp.int32, (tq, dh), 1)
    lane_k = lax.broadcasted_iota(jnp.int32, (tk, dh), 1)
    q = jnp.concatenate([q_ref[0], jnp.where(lane_q < FOX_AUG, 1.0, 0.0).astype(BF16)], axis=1)
    v_ones = jnp.where(lane_k == 0, 1.0, 0.0).astype(BF16)

    def logits(j):
        start = pl.multiple_of(j * tk, tk)
        k = jnp.concatenate([k_ref[0, pl.ds(start, tk), :], kx_ref[0, pl.ds(start, tk), :]], axis=1)
        return lax.dot_general(q, k, (((1,), (1,)), ((), ())), preferred_element_type=F32)

    def softmax(slot, masked):
        s = s_buf[slot]
        if masked:
            r_i = lax.broadcasted_iota(jnp.int32, s.shape, 0)
            c_i = lax.broadcasted_iota(jnp.int32, s.shape, 1)
            s = jnp.where(c_i <= r_i, s, NEG_BIG)
        m = m_ref[...]
        m_new = jnp.maximum(m, jnp.broadcast_to(jnp.max(s, axis=-1, keepdims=True), m.shape))
        for c in range(tk // dh):
            p_buf[slot, :, c * dh:(c + 1) * dh] = jnp.exp2(s[:, c * dh:(c + 1) * dh] - m_new).astype(BF16)
        a_buf[slot] = jnp.exp2(m - m_new)
        m_ref[...] = m_new

    def accumulate(slot, j):
        start = pl.multiple_of(j * tk, tk)
        v = jnp.concatenate([v_ref[0, pl.ds(start, tk), :], v_ones], axis=1)
        pv = jnp.dot(p_buf[slot], v, preferred_element_type=F32)
        a = a_buf[slot]
        for c in range(2):
            acc_ref[:, c * dh:(c + 1) * dh] = a * acc_ref[:, c * dh:(c + 1) * dh] + pv[:, c * dh:(c + 1) * dh]

    n = qi
    m_ref[...] = jnp.full(m_ref.shape, NEG_BIG, F32)
    acc_ref[...] = jnp.zeros(acc_ref.shape, F32)
    s_buf[0] = logits(qi)
    s_buf[1] = logits(0)
    softmax(0, True)

    def pair(i, carry):
        t = 2 * i + 1
        s_buf[0] = logits(t)
        softmax(1, False)
        accumulate(0, jnp.where(i == 0, qi, t - 2))
        s_buf[1] = logits(t + 1)
        softmax(0, False)
        accumulate(1, t - 1)
        return carry

    lax.fori_loop(0, n // 2, pair, 0)

    @pl.when(n % 2 == 1)
    def _():
        softmax(1, False)
        accumulate(0, jnp.where(n == 1, qi, n - 2))
        accumulate(1, n - 1)

    @pl.when(n % 2 == 0)
    def _():
        accumulate(0, jnp.where(n == 0, qi, n - 1))

    acc = acc_ref[...]
    o_ref[0] = (acc[:, :dh] / acc[:, dh:dh + 1]).astype(o_ref.dtype)


def _fox(z, kx):
    b, s, _ = z.shape
    tq = min(512, s)
    tk = min(512, s)
    h = N_FOX_HEADS
    return pl.pallas_call(
        functools.partial(_fox_kernel, tq=tq, tk=tk),
        grid=(b, h, s // tq),
        in_specs=[pl.BlockSpec((1, tq, FOX_HEAD_DIM), lambda bi, hi, qi: (bi, qi, hi)),
                  pl.BlockSpec((1, s, FOX_HEAD_DIM), lambda bi, hi, qi: (bi, 0, h + hi)),
                  pl.BlockSpec((1, s, GATE_LANES), lambda bi, hi, qi: (bi, 0, hi)),
                  pl.BlockSpec((1, s, FOX_HEAD_DIM), lambda bi, hi, qi: (bi, 0, 2 * h + hi))],
        out_specs=pl.BlockSpec((1, tq, FOX_HEAD_DIM), lambda bi, hi, qi: (bi, qi, hi)),
        out_shape=jax.ShapeDtypeStruct((b, s, FOX_WIDTH), BF16),
        scratch_shapes=[pltpu.VMEM((2, tq, tk), F32),
                        pltpu.VMEM((2, tq, tk), BF16),
                        pltpu.VMEM((2, tq, FOX_HEAD_DIM), F32),
                        pltpu.VMEM((tq, FOX_HEAD_DIM), F32),
                        pltpu.VMEM((tq, 2 * FOX_HEAD_DIM), F32)],
        compiler_params=_cparams(("parallel", "parallel", "arbitrary")),
        name="fox_attn",
    )(z, z, kx, z)


def _mlstm_kernel(qk_ref, v_ref, o_ref, gn_ref, gt_ref, cw_ref, cb_ref, ng_ref, y_ref,
                  c_st, n_st, m_st, tail, ubuf):
    c = pl.program_id(1)
    L = qk_ref.shape[1]
    dk, dv = MLSTM_QK_DIM, MLSTM_V_DIM

    @pl.when(c == 0)
    def _():
        c_st[...] = jnp.zeros_like(c_st)
        n_st[...] = jnp.zeros_like(n_st)
        m_st[...] = jnp.zeros_like(m_st)
        tail[...] = jnp.zeros_like(tail)

    u = qk_ref[0].astype(F32)
    ubuf[0:8, :] = tail[...]
    ubuf[8:8 + L, :] = u
    tail[...] = u[L - 8:L, :]
    conv = cb_ref[...] + cw_ref[3:4, :] * u
    for d in range(1, CONV_WIDTH):
        conv = conv + cw_ref[3 - d:4 - d, :] * ubuf[8 - d:8 - d + L, :]
    qk = conv * _sigmoid(conv)

    gn = gn_ref[0]
    gt = gt_ref[0]
    r_i = lax.broadcasted_iota(jnp.int32, (L, L), 0)
    c_i = lax.broadcasted_iota(jnp.int32, (L, L), 1)
    causal = c_i <= r_i

    for h in range(N_MLSTM_HEADS):
        qf = qk[:, h * dk:(h + 1) * dk]
        q = qf.astype(BF16)
        kf = qk[:, MLSTM_QK_WIDTH + h * dk:MLSTM_QK_WIDTH + (h + 1) * dk] * (dk ** -0.5)
        v = v_ref[0, :, h * dv:(h + 1) * dv]
        bcol = gn[:, G_MF + h:G_MF + h + 1]
        icol = gn[:, G_MI + h:G_MI + h + 1]
        brow = gt[G_MF + h:G_MF + h + 1, :]
        irow = gt[G_MI + h:G_MI + h + 1, :]
        m_prev = m_st[h][:, 0:1]
        c_prev = c_st[h]
        n_prev = n_st[h]

        m_inter = bcol + m_prev
        dlog = jnp.where(causal, bcol - brow + irow, NEG_BIG)
        m_t = jnp.maximum(m_inter, jnp.max(dlog, axis=-1, keepdims=True))
        s = lax.dot_general(q, kf.astype(BF16), (((1,), (1,)), ((), ())), preferred_element_type=F32)
        s = s * jnp.exp(dlog - m_t)
        inter = jnp.exp(m_inter - m_t)
        num = (inter * jnp.dot(q, c_prev.astype(BF16), preferred_element_type=F32)
               + jnp.dot(s.astype(BF16), v, preferred_element_type=F32))
        den = inter * jnp.sum(qf * n_prev, axis=-1, keepdims=True) + jnp.sum(s, axis=-1, keepdims=True)
        hh = num / jnp.maximum(jnp.abs(den), jnp.exp(-m_t))

        b_last = bcol[L - 1:L, :]
        wlog = b_last - bcol + icol
        m_new = jnp.maximum(b_last + m_prev, jnp.max(wlog, axis=0, keepdims=True))
        decay = jnp.exp(b_last + m_prev - m_new)
        wk = kf * jnp.exp(wlog - m_new)
        c_st[h] = decay * c_prev + jnp.dot(wk.T.astype(BF16), v, preferred_element_type=F32)
        n_st[h] = decay * n_prev + jnp.sum(wk, axis=0, keepdims=True)
        m_st[h] = jnp.broadcast_to(m_new, (1, GATE_LANES))

        hn = hh * lax.rsqrt(jnp.mean(hh * hh, axis=-1, keepdims=True) + LN_EPS)
        hn = hn * ng_ref[:, h * dv:(h + 1) * dv]
        og = _sigmoid(o_ref[0, :, h * dv:(h + 1) * dv].astype(F32))
        y_ref[0, :, h * dv:(h + 1) * dv] = (og * hn).astype(y_ref.dtype)


def _mlstm(z, gates_n, gates_t, conv_w, conv_b, norm_g):
    b, s, _ = z.shape
    L = min(MLSTM_CHUNK, s)
    wq = 2 * MLSTM_QK_WIDTH
    assert wq == MLSTM_WIDTH == FOX_WIDTH
    base = 3 * FOX_WIDTH // wq
    return pl.pallas_call(
        _mlstm_kernel,
        grid=(b, s // L),
        in_specs=[pl.BlockSpec((1, L, wq), lambda bi, ci: (bi, ci, base)),
                  pl.BlockSpec((1, L, MLSTM_WIDTH), lambda bi, ci: (bi, ci, base + 1)),
                  pl.BlockSpec((1, L, MLSTM_WIDTH), lambda bi, ci: (bi, ci, base + 2)),
                  pl.BlockSpec((1, L, GATE_LANES), lambda bi, ci: (bi, ci, 0)),
                  pl.BlockSpec((1, G_ROWS, L), lambda bi, ci: (bi, 0, ci)),
                  pl.BlockSpec((CONV_WIDTH, wq), lambda bi, ci: (0, 0)),
                  pl.BlockSpec((1, wq), lambda bi, ci: (0, 0)),
                  pl.BlockSpec((1, MLSTM_WIDTH), lambda bi, ci: (0, 0))],
        out_specs=pl.BlockSpec((1, L, MLSTM_WIDTH), lambda bi, ci: (bi, ci, 0)),
        out_shape=jax.ShapeDtypeStruct((b, s, MLSTM_WIDTH), BF16),
        scratch_shapes=[pltpu.VMEM((N_MLSTM_HEADS, MLSTM_QK_DIM, MLSTM_V_DIM), F32),
                        pltpu.VMEM((N_MLSTM_HEADS, 1, MLSTM_QK_DIM), F32),
                        pltpu.VMEM((N_MLSTM_HEADS, 1, GATE_LANES), F32),
                        pltpu.VMEM((8, wq), F32),
                        pltpu.VMEM((L + 8, wq), F32)],
        compiler_params=_cparams(("parallel", "arbitrary")),
        name="mlstm",
    )(z, z, z, gates_n, gates_t, conv_w, conv_b, norm_g)


def _outproj_kernel(yf_ref, ym_ref, x_ref, gt_ref, w_ref, lng_ref, lnb_ref, o_ref):
    nf = yf_ref.shape[2]
    hmix = (jnp.dot(yf_ref[0], w_ref[0:nf, :], preferred_element_type=F32)
            + jnp.dot(ym_ref[0], w_ref[nf:, :], preferred_element_type=F32))
    y = ALPHA * x_ref[0] + (1.0 + gt_ref[0]) * hmix
    o_ref[0] = _ln(y) * lng_ref[...] + lnb_ref[...]


def _outproj(y_fox, y_mlstm, x, ada3, idx, w_out, ln_g, ln_b):
    b, s, d = x.shape
    tm = min(512, s)
    return pl.pallas_call(
        _outproj_kernel,
        grid=(b, s // tm),
        in_specs=[pl.BlockSpec((1, tm, FOX_WIDTH), lambda bi, i: (bi, i, 0)),
                  pl.BlockSpec((1, tm, MLSTM_WIDTH), lambda bi, i: (bi, i, 0)),
                  pl.BlockSpec((1, tm, d), lambda bi, i: (bi, i, 0)),
                  pl.BlockSpec((1, 1, d), lambda bi, i: (bi, 0, idx)),
                  pl.BlockSpec(w_out.shape, lambda bi, i: (0, 0)),
                  pl.BlockSpec((1, d), lambda bi, i: (0, 0)),
                  pl.BlockSpec((1, d), lambda bi, i: (0, 0))],
        out_specs=pl.BlockSpec((1, tm, d), lambda bi, i: (bi, i, 0)),
        out_shape=jax.ShapeDtypeStruct((b, s, d), F32),
        compiler_params=_cparams(("parallel", "parallel")),
        name="out_proj",
    )(y_fox, y_mlstm, x, ada3, w_out, ln_g, ln_b)


def _layer(x, c, w_ada, b_ada, ffn1_w_in, ffn1_w_out, ln1_g, ln1_b, w_in, fox_f_bias, mlstm_conv_w,
           mlstm_conv_b, mlstm_i_bias, mlstm_f_bias, mlstm_norm_g, w_out, ln2_g, ln2_b,
           ffn2_w_in, ffn2_w_out, ln3_g, ln3_b):
    b, s, d = x.shape
    row = lambda a: a.reshape(1, -1)

    c_pad = jnp.zeros((8, d), F32).at[:b].set(c)
    ada3 = _ada(c_pad, w_ada, row(b_ada)).reshape(8, 1, N_ADA * d)

    x = _ffn(x, ada3, 0, ffn1_w_in.astype(BF16), ffn1_w_out.astype(BF16), row(ln1_g), row(ln1_b))

    w_mix = jnp.concatenate([w_in[:, :COL_FOX_F], w_in[:, COL_MLSTM_Q:COL_MLSTM_I], w_in[:, COL_MLSTM_O:]],
                            axis=1).astype(BF16)
    w_gate = jnp.concatenate([w_in[:, COL_FOX_F:COL_MLSTM_Q], w_in[:, COL_MLSTM_I:COL_MLSTM_O],
                              jnp.zeros((d, GATE_LANES - G_ROWS), F32)], axis=1).astype(BF16)
    gate_bias = jnp.concatenate([fox_f_bias, mlstm_i_bias, mlstm_f_bias,
                                 jnp.zeros((GATE_LANES - G_ROWS,), F32)]).reshape(1, GATE_LANES)
    z, zg = _inproj(x, ada3, 3, w_mix, w_gate)
    gates_n, gates_t, kx = _gates(zg, gate_bias)
    y_fox = _fox(z, kx)
    y_mlstm = _mlstm(z, gates_n, gates_t, mlstm_conv_w, row(mlstm_conv_b), row(mlstm_norm_g))
    x = _outproj(y_fox, y_mlstm, x, ada3, 5, w_out.astype(BF16), row(ln2_g), row(ln2_b))

    x = _ffn(x, ada3, 6, ffn2_w_in.astype(BF16), ffn2_w_out.astype(BF16), row(ln3_g), row(ln3_b))
    return x


def kernel(x, c, w_ada, b_ada, ffn1_w_in, ffn1_w_out, ln1_g, ln1_b, w_in, fox_f_bias, mlstm_conv_w,
           mlstm_conv_b, mlstm_i_bias, mlstm_f_bias, mlstm_norm_g, w_out, ln2_g, ln2_b,
           ffn2_w_in, ffn2_w_out, ln3_g, ln3_b):
    for l in range(DEPTH):
        x = _layer(x, c, w_ada[l], b_ada[l], ffn1_w_in[l], ffn1_w_out[l], ln1_g[l], ln1_b[l],
                   w_in[l], fox_f_bias[l], mlstm_conv_w[l], mlstm_conv_b[l], mlstm_i_bias[l],
                   mlstm_f_bias[l], mlstm_norm_g[l], w_out[l], ln2_g[l], ln2_b[l],
                   ffn2_w_in[l], ffn2_w_out[l], ln3_g[l], ln3_b[l])
    return x
```

```python
import functools

import jax
import jax.numpy as jnp
from jax import lax
from jax.experimental import pallas as pl
from jax.experimental.pallas import tpu as pltpu

F32 = jnp.float32
BF16 = jnp.bfloat16

D_MODEL = 2048
DEPTH = 1
N_FOX_HEADS = 8
FOX_HEAD_DIM = 128
FOX_WIDTH = N_FOX_HEADS * FOX_HEAD_DIM
N_MLSTM_HEADS = 4
MLSTM_V_DIM = 256
MLSTM_QK_DIM = 128
MLSTM_WIDTH = N_MLSTM_HEADS * MLSTM_V_DIM
MLSTM_QK_WIDTH = N_MLSTM_HEADS * MLSTM_QK_DIM
CONV_WIDTH = 4
D_FF = 5632
N_ADA = 9
ALPHA = (2 * DEPTH) ** 0.25
LN_EPS = 1e-5

COL_FOX_F = 3 * FOX_WIDTH
COL_MLSTM_Q = COL_FOX_F + N_FOX_HEADS
COL_MLSTM_I = COL_MLSTM_Q + 2 * MLSTM_QK_WIDTH + MLSTM_WIDTH
COL_MLSTM_O = COL_MLSTM_I + 2 * N_MLSTM_HEADS
IN_WIDTH = COL_MLSTM_O + MLSTM_WIDTH

GATE_LANES = 128
G_FOX = 0
G_MI = N_FOX_HEADS
G_MF = G_MI + N_MLSTM_HEADS
G_ROWS = 16
MIX_COLS = 3 * FOX_WIDTH + 2 * MLSTM_QK_WIDTH + 2 * MLSTM_WIDTH

MLSTM_CHUNK = 256
NEG_BIG = -1e30
LOG2E = 1.4426950408889634
FOX_AUG = 3
FOX_UNROLL = 4
VMEM_LIMIT = 56 * 1024 * 1024


def _cparams(sem):
    return pltpu.CompilerParams(dimension_semantics=sem, vmem_limit_bytes=VMEM_LIMIT)


def _ln(x):
    mu = jnp.mean(x, axis=-1, keepdims=True)
    xc = x - mu
    var = jnp.mean(xc * xc, axis=-1, keepdims=True)
    return xc * lax.rsqrt(var + LN_EPS)


def _sigmoid(x):
    return 1.0 / (1.0 + jnp.exp(-x))


def _log_sigmoid(x):
    return jnp.minimum(x, 0.0) - jnp.log1p(jnp.exp(-jnp.abs(x)))


def _ada_kernel(c_ref, w_ref, b_ref, o_ref):
    c = c_ref[...]
    s = (c * _sigmoid(c)).astype(BF16)
    o_ref[...] = jnp.dot(s, w_ref[...].astype(BF16), preferred_element_type=F32) + b_ref[...]


def _ada(c_pad, w_ada, b_ada):
    rows, d = c_pad.shape
    n = w_ada.shape[1]
    tn = 1024
    return pl.pallas_call(
        _ada_kernel,
        grid=(n // tn,),
        in_specs=[pl.BlockSpec((rows, d), lambda j: (0, 0)),
                  pl.BlockSpec((d, tn), lambda j: (0, j)),
                  pl.BlockSpec((1, tn), lambda j: (0, j))],
        out_specs=pl.BlockSpec((rows, tn), lambda j: (0, j)),
        out_shape=jax.ShapeDtypeStruct((rows, n), F32),
        compiler_params=_cparams(("arbitrary",)),
        name="ada_proj",
    )(c_pad, w_ada, b_ada)


def _ffn_kernel(x_ref, sh_ref, sc_ref, gt_ref, wg_ref, wu_ref, wo_ref, lng_ref, lnb_ref, o_ref, h_ref):
    k = pl.program_id(2)
    last = pl.num_programs(2) - 1

    def swiglu_part(h):
        g = jnp.dot(h, wg_ref[...], preferred_element_type=F32)
        u = jnp.dot(h, wu_ref[...], preferred_element_type=F32)
        a = (g * _sigmoid(g) * u).astype(BF16)
        return jnp.dot(a, wo_ref[...], preferred_element_type=F32)

    @pl.when(k == 0)
    def _():
        h = (_ln(x_ref[0]) * (1.0 + sc_ref[0]) + sh_ref[0]).astype(BF16)
        h_ref[...] = h
        o_ref[0] = swiglu_part(h)

    @pl.when((k > 0) & (k < last))
    def _():
        o_ref[0] += swiglu_part(h_ref[...])

    @pl.when(k == last)
    def _():
        y = ALPHA * x_ref[0] + (0.5 * (1.0 + gt_ref[0])) * (o_ref[0] + swiglu_part(h_ref[...]))
        o_ref[0] = _ln(y) * lng_ref[...] + lnb_ref[...]


def _ffn(x, ada3, idx, w_in, w_out, ln_g, ln_b):
    b, s, d = x.shape
    dff = w_out.shape[0]
    tm = min(512, s)
    tf = 512
    nk = dff // tf
    vec = lambda j: pl.BlockSpec((1, 1, d), lambda bi, i, k: (bi, 0, j))
    return pl.pallas_call(
        _ffn_kernel,
        grid=(b, s // tm, nk),
        in_specs=[pl.BlockSpec((1, tm, d), lambda bi, i, k: (bi, i, 0)),
                  vec(idx), vec(idx + 1), vec(idx + 2),
                  pl.BlockSpec((d, tf), lambda bi, i, k: (0, k)),
                  pl.BlockSpec((d, tf), lambda bi, i, k: (0, k + nk)),
                  pl.BlockSpec((tf, d), lambda bi, i, k: (k, 0)),
                  pl.BlockSpec((1, d), lambda bi, i, k: (0, 0)),
                  pl.BlockSpec((1, d), lambda bi, i, k: (0, 0))],
        out_specs=pl.BlockSpec((1, tm, d), lambda bi, i, k: (bi, i, 0)),
        out_shape=jax.ShapeDtypeStruct((b, s, d), F32),
        scratch_shapes=[pltpu.VMEM((tm, d), BF16)],
        compiler_params=_cparams(("parallel", "parallel", "arbitrary")),
        name="ffn",
    )(x, ada3, ada3, ada3, w_in, w_in, w_out, ln_g, ln_b)


def _inproj_kernel(x_ref, sh_ref, sc_ref, w_ref, wgate_ref, z_ref, zg_ref, h_ref, *, n_scaled, q_scale):
    j = pl.program_id(2)

    def project(h):
        z = jnp.dot(h, w_ref[...], preferred_element_type=F32)
        z = z * jnp.where(j < n_scaled, q_scale, 1.0).astype(F32)
        z_ref[0] = z.astype(BF16)

    @pl.when(j == 0)
    def _():
        h = (_ln(x_ref[0]) * (1.0 + sc_ref[0]) + sh_ref[0]).astype(BF16)
        h_ref[...] = h
        zg_ref[0] = jnp.dot(h, wgate_ref[...], preferred_element_type=F32)
        project(h)

    @pl.when(j > 0)
    def _():
        project(h_ref[...])


def _inproj(x, ada3, idx, w_mix, w_gate):
    b, s, d = x.shape
    n = w_mix.shape[1]
    tm = min(1024, s)
    tn = 512
    vec = lambda j: pl.BlockSpec((1, 1, d), lambda bi, i, jj: (bi, 0, j))
    kern = functools.partial(_inproj_kernel, n_scaled=FOX_WIDTH // tn, q_scale=FOX_HEAD_DIM ** -0.5 * LOG2E)
    return pl.pallas_call(
        kern,
        grid=(b, s // tm, n // tn),
        in_specs=[pl.BlockSpec((1, tm, d), lambda bi, i, j: (bi, i, 0)),
                  vec(idx), vec(idx + 1),
                  pl.BlockSpec((d, tn), lambda bi, i, j: (0, j)),
                  pl.BlockSpec((d, GATE_LANES), lambda bi, i, j: (0, 0))],
        out_specs=[pl.BlockSpec((1, tm, tn), lambda bi, i, j: (bi, i, j)),
                   pl.BlockSpec((1, tm, GATE_LANES), lambda bi, i, j: (bi, i, 0))],
        out_shape=[jax.ShapeDtypeStruct((b, s, n), BF16),
                   jax.ShapeDtypeStruct((b, s, GATE_LANES), F32)],
        scratch_shapes=[pltpu.VMEM((tm, d), BF16)],
        compiler_params=_cparams(("parallel", "parallel", "arbitrary")),
        name="in_proj",
    )(x, ada3, ada3, w_mix, w_gate)


def _split3(v):
    hi = v.astype(BF16)
    r = v - hi.astype(F32)
    mid = r.astype(BF16)
    lo = (r - mid.astype(F32)).astype(BF16)
    return hi, mid, lo


def _gate_kernel(zg_ref, bias_ref, nat_ref, t_ref, kx_ref, carry_ref, *, n_dead):
    i = pl.program_id(1)
    n_live = pl.num_programs(1) - n_dead

    @pl.when(i == 0)
    def _():
        carry_ref[...] = jnp.zeros_like(carry_ref)

    col = lax.broadcasted_iota(jnp.int32, zg_ref.shape[1:], 1)

    @pl.when(i < n_live)
    def _():
        z = zg_ref[0] + bias_ref[...]
        rows = z.shape[0]
        is_in_gate = (col >= G_MI) & (col < G_MF)
        v = jnp.where(is_in_gate, z, _log_sigmoid(z))
        r_i = lax.broadcasted_iota(jnp.int32, (rows, rows), 0)
        c_i = lax.broadcasted_iota(jnp.int32, (rows, rows), 1)
        tri = jnp.where(r_i >= c_i, 1.0, 0.0).astype(BF16)
        hi, mid, lo = _split3(v)
        csum = (jnp.dot(tri, hi, preferred_element_type=F32)
                + jnp.dot(tri, mid, preferred_element_type=F32)
                + jnp.dot(tri, lo, preferred_element_type=F32))
        running = csum + carry_ref[...]
        carry_ref[...] = running[rows - 1:rows, :]
        out = jnp.where(col < G_MI, running, jnp.where(is_in_gate, v, csum))
        nat_ref[0] = out
        t_ref[0] = out.T[:G_ROWS, :]
        for h in range(N_FOX_HEADS):
            parts = _split3(jnp.broadcast_to(running[:, h:h + 1] * (-LOG2E), z.shape))
            tile = jnp.zeros(z.shape, F32)
            for t in reversed(range(FOX_AUG)):
                tile = jnp.where(col == t, parts[t].astype(F32), tile)
            kx_ref[0, :, h * GATE_LANES:(h + 1) * GATE_LANES] = tile.astype(BF16)

    @pl.when(i >= n_live)
    def _():
        tile = jnp.where(col == FOX_AUG, NEG_BIG, 0.0).astype(BF16)
        for h in range(N_FOX_HEADS):
            kx_ref[0, :, h * GATE_LANES:(h + 1) * GATE_LANES] = tile


def _gates(zg, bias, pad_rows):
    b, s, _ = zg.shape
    tl = min(MLSTM_CHUNK, s)
    n_live, n_dead = s // tl, pad_rows // tl
    live = lambda i: jnp.minimum(i, n_live - 1)
    return pl.pallas_call(
        functools.partial(_gate_kernel, n_dead=n_dead),
        grid=(b, n_live + n_dead),
        in_specs=[pl.BlockSpec((1, tl, GATE_LANES), lambda bi, i: (bi, live(i), 0)),
                  pl.BlockSpec((1, GATE_LANES), lambda bi, i: (0, 0))],
        out_specs=[pl.BlockSpec((1, tl, GATE_LANES), lambda bi, i: (bi, live(i), 0)),
                   pl.BlockSpec((1, G_ROWS, tl), lambda bi, i: (bi, 0, live(i))),
                   pl.BlockSpec((1, tl, N_FOX_HEADS * GATE_LANES), lambda bi, i: (bi, i, 0))],
        out_shape=[jax.ShapeDtypeStruct((b, s, GATE_LANES), F32),
                   jax.ShapeDtypeStruct((b, G_ROWS, s), F32),
                   jax.ShapeDtypeStruct((b, s + pad_rows, N_FOX_HEADS * GATE_LANES), BF16)],
        scratch_shapes=[pltpu.VMEM((1, GATE_LANES), F32)],
        compiler_params=_cparams(("parallel", "arbitrary")),
        name="gates",
    )(zg, bias)


def _fox_kernel(q_ref, k_ref, kx_ref, v_ref, o_ref, qt_ref, s_buf, p_buf, a_buf, m_ref, l_ref, acc_ref, *, tq, tk):
    qi = pl.program_id(2)
    dh = FOX_HEAD_DIM
    row_q = lax.broadcasted_iota(jnp.int32, (dh, tq), 0)
    qt_ref[0:dh, :] = q_ref[0].astype(F32).T.astype(BF16)
    qt_ref[dh:2 * dh, :] = jnp.where(row_q <= FOX_AUG, 1.0, 0.0).astype(BF16)

    n = qi
    pad_block = k_ref.shape[1] // tk

    def block_of(t):
        return jnp.minimum(t - 1, n)

    def logits(slot, t, j):
        start = pl.multiple_of(j * tk, tk)
        start_x = pl.multiple_of(jnp.where(t > n, pad_block, j) * tk, tk)
        k = jnp.concatenate([k_ref[0, pl.ds(start, tk), :], kx_ref[0, pl.ds(start_x, tk), :]], axis=1)
        s_buf[slot] = jnp.dot(k, qt_ref[...], preferred_element_type=F32)

    def softmax(slot, masked):
        s = s_buf[slot]
        if masked:
            key_i = lax.broadcasted_iota(jnp.int32, s.shape, 0)
            qry_i = lax.broadcasted_iota(jnp.int32, s.shape, 1)
            s = jnp.where(key_i <= qry_i, s, NEG_BIG)
        m = m_ref[...]
        m_new = jnp.maximum(m, jnp.max(s, axis=0, keepdims=True))
        p = jnp.exp2(s - m_new)
        a = jnp.exp2(m - m_new)
        l_ref[...] = a * l_ref[...] + jnp.sum(p, axis=0, keepdims=True)
        p_buf[slot] = p.astype(BF16)
        a_buf[slot] = a
        m_ref[...] = m_new

    def accumulate(slot, j):
        start = pl.multiple_of(j * tk, tk)
        v = v_ref[0, pl.ds(start, tk), :]
        pv = lax.dot_general(v, p_buf[slot], (((0,), (0,)), ((), ())), preferred_element_type=F32)
        acc_ref[...] = a_buf[slot] * acc_ref[...] + pv

    m_ref[...] = jnp.full(m_ref.shape, NEG_BIG, F32)
    l_ref[...] = jnp.zeros(l_ref.shape, F32)
    acc_ref[...] = jnp.zeros(acc_ref.shape, F32)
    logits(0, 0, qi)
    logits(1, 1, block_of(1))

    def group(t0, first):
        for u in range(FOX_UNROLL):
            t = t0 + u
            logits((u + 2) % FOX_UNROLL, t + 2, block_of(t + 2))
            softmax(u, first and u == 0)
            if not (first and u < 2):
                accumulate((u + 2) % FOX_UNROLL, qi if (first and u == 2) else block_of(t - 2))

    group(0, True)

    def body(g, carry):
        group(g * FOX_UNROLL, False)
        return carry

    n_groups = (n + FOX_UNROLL) // FOX_UNROLL
    lax.fori_loop(1, n_groups, body, 0)
    t_end = n_groups * FOX_UNROLL
    accumulate(FOX_UNROLL - 2, block_of(t_end - 2))
    accumulate(FOX_UNROLL - 1, block_of(t_end - 1))
    o_ref[0] = (acc_ref[...] / l_ref[...]).T.astype(o_ref.dtype)


def _fox_tile(s):
    return min(512, s)


def _fox(z, kx):
    b, s, _ = z.shape
    tq = tk = _fox_tile(s)
    h = N_FOX_HEADS
    return pl.pallas_call(
        functools.partial(_fox_kernel, tq=tq, tk=tk),
        grid=(b, h, s // tq),
        in_specs=[pl.BlockSpec((1, tq, FOX_HEAD_DIM), lambda bi, hi, qi: (bi, qi, hi)),
                  pl.BlockSpec((1, s, FOX_HEAD_DIM), lambda bi, hi, qi: (bi, 0, h + hi)),
                  pl.BlockSpec((1, s + tk, GATE_LANES), lambda bi, hi, qi: (bi, 0, hi)),
                  pl.BlockSpec((1, s, FOX_HEAD_DIM), lambda bi, hi, qi: (bi, 0, 2 * h + hi))],
        out_specs=pl.BlockSpec((1, tq, FOX_HEAD_DIM), lambda bi, hi, qi: (bi, qi, hi)),
        out_shape=jax.ShapeDtypeStruct((b, s, FOX_WIDTH), BF16),
        scratch_shapes=[pltpu.VMEM((2 * FOX_HEAD_DIM, tq), BF16),
                        pltpu.VMEM((FOX_UNROLL, tk, tq), F32),
                        pltpu.VMEM((FOX_UNROLL, tk, tq), BF16),
                        pltpu.VMEM((FOX_UNROLL, 1, tq), F32),
                        pltpu.VMEM((1, tq), F32),
                        pltpu.VMEM((1, tq), F32),
                        pltpu.VMEM((FOX_HEAD_DIM, tq), F32)],
        compiler_params=_cparams(("parallel", "parallel", "arbitrary")),
        name="fox_attn",
    )(z, z, kx, z)


def _mlstm_kernel(qk_ref, v_ref, o_ref, gn_ref, gt_ref, cw_ref, cb_ref, ng_ref, y_ref,
                  c_st, n_st, m_st, tail, ubuf):
    c = pl.program_id(1)
    L = qk_ref.shape[1]
    dk, dv = MLSTM_QK_DIM, MLSTM_V_DIM

    @pl.when(c == 0)
    def _():
        c_st[...] = jnp.zeros_like(c_st)
        n_st[...] = jnp.zeros_like(n_st)
        m_st[...] = jnp.zeros_like(m_st)
        tail[...] = jnp.zeros_like(tail)

    u = qk_ref[0].astype(F32)
    ubuf[0:8, :] = tail[...]
    ubuf[8:8 + L, :] = u
    tail[...] = u[L - 8:L, :]
    conv = cb_ref[...] + cw_ref[3:4, :] * u
    for d in range(1, CONV_WIDTH):
        conv = conv + cw_ref[3 - d:4 - d, :] * ubuf[8 - d:8 - d + L, :]
    qk = conv * _sigmoid(conv)

    gn = gn_ref[0]
    gt = gt_ref[0]
    r_i = lax.broadcasted_iota(jnp.int32, (L, L), 0)
    c_i = lax.broadcasted_iota(jnp.int32, (L, L), 1)
    causal = c_i <= r_i

    for h in range(N_MLSTM_HEADS):
        qf = qk[:, h * dk:(h + 1) * dk]
        q = qf.astype(BF16)
        kf = qk[:, MLSTM_QK_WIDTH + h * dk:MLSTM_QK_WIDTH + (h + 1) * dk] * (dk ** -0.5)
        v = v_ref[0, :, h * dv:(h + 1) * dv]
        bcol = gn[:, G_MF + h:G_MF + h + 1]
        icol = gn[:, G_MI + h:G_MI + h + 1]
        brow = gt[G_MF + h:G_MF + h + 1, :]
        irow = gt[G_MI + h:G_MI + h + 1, :]
        m_prev = m_st[h][:, 0:1]
        c_prev = c_st[h]
        n_prev = n_st[h]

        m_inter = bcol + m_prev
        dlog = jnp.where(causal, bcol - brow + irow, NEG_BIG)
        m_t = jnp.maximum(m_inter, jnp.max(dlog, axis=-1, keepdims=True))
        s = lax.dot_general(q, kf.astype(BF16), (((1,), (1,)), ((), ())), preferred_element_type=F32)
        s = s * jnp.exp(dlog - m_t)
        inter = jnp.exp(m_inter - m_t)
        num = (inter * jnp.dot(q, c_prev.astype(BF16), preferred_element_type=F32)
               + jnp.dot(s.astype(BF16), v, preferred_element_type=F32))
        den = inter * jnp.sum(qf * n_prev, axis=-1, keepdims=True) + jnp.sum(s, axis=-1, keepdims=True)
        hh = num / jnp.maximum(jnp.abs(den), jnp.exp(-m_t))

        b_last = bcol[L - 1:L, :]
        wlog = b_last - bcol + icol
        m_new = jnp.maximum(b_last + m_prev, jnp.max(wlog, axis=0, keepdims=True))
        decay = jnp.exp(b_last + m_prev - m_new)
        wk = kf * jnp.exp(wlog - m_new)
        c_st[h] = decay * c_prev + jnp.dot(wk.T.astype(BF16), v, preferred_element_type=F32)
        n_st[h] = decay * n_prev + jnp.sum(wk, axis=0, keepdims=True)
        m_st[h] = jnp.broadcast_to(m_new, (1, GATE_LANES))

        hn = hh * lax.rsqrt(jnp.mean(hh * hh, axis=-1, keepdims=True) + LN_EPS)
        hn = hn * ng_ref[:, h * dv:(h + 1) * dv]
        og = _sigmoid(o_ref[0, :, h * dv:(h + 1) * dv].astype(F32))
        y_ref[0, :, h * dv:(h + 1) * dv] = (og * hn).astype(y_ref.dtype)


def _mlstm(z, gates_n, gates_t, conv_w, conv_b, norm_g):
    b, s, _ = z.shape
    L = min(MLSTM_CHUNK, s)
    wq = 2 * MLSTM_QK_WIDTH
    assert wq == MLSTM_WIDTH == FOX_WIDTH
    base = 3 * FOX_WIDTH // wq
    return pl.pallas_call(
        _mlstm_kernel,
        grid=(b, s // L),
        in_specs=[pl.BlockSpec((1, L, wq), lambda bi, ci: (bi, ci, base)),
                  pl.BlockSpec((1, L, MLSTM_WIDTH), lambda bi, ci: (bi, ci, base + 1)),
                  pl.BlockSpec((1, L, MLSTM_WIDTH), lambda bi, ci: (bi, ci, base + 2)),
                  pl.BlockSpec((1, L, GATE_LANES), lambda bi, ci: (bi, ci, 0)),
                  pl.BlockSpec((1, G_ROWS, L), lambda bi, ci: (bi, 0, ci)),
                  pl.BlockSpec((CONV_WIDTH, wq), lambda bi, ci: (0, 0)),
                  pl.BlockSpec((1, wq), lambda bi, ci: (0, 0)),
                  pl.BlockSpec((1, MLSTM_WIDTH), lambda bi, ci: (0, 0))],
        out_specs=pl.BlockSpec((1, L, MLSTM_WIDTH), lambda bi, ci: (bi, ci, 0)),
        out_shape=jax.ShapeDtypeStruct((b, s, MLSTM_WIDTH), BF16),
        scratch_shapes=[pltpu.VMEM((N_MLSTM_HEADS, MLSTM_QK_DIM, MLSTM_V_DIM), F32),
                        pltpu.VMEM((N_MLSTM_HEADS, 1, MLSTM_QK_DIM), F32),
                        pltpu.VMEM((N_MLSTM_HEADS, 1, GATE_LANES), F32),
                        pltpu.VMEM((8, wq), F32),
                        pltpu.VMEM((L + 8, wq), F32)],
        compiler_params=_cparams(("parallel", "arbitrary")),
        name="mlstm",
    )(z, z, z, gates_n, gates_t, conv_w, conv_b, norm_g)


def _outproj_kernel(yf_ref, ym_ref, x_ref, gt_ref, w_ref, lng_ref, lnb_ref, o_ref):
    nf = yf_ref.shape[2]
    hmix = (jnp.dot(yf_ref[0], w_ref[0:nf, :], preferred_element_type=F32)
            + jnp.dot(ym_ref[0], w_ref[nf:, :], preferred_element_type=F32))
    y = ALPHA * x_ref[0] + (1.0 + gt_ref[0]) * hmix
    o_ref[0] = _ln(y) * lng_ref[...] + lnb_ref[...]


def _outproj(y_fox, y_mlstm, x, ada3, idx, w_out, ln_g, ln_b):
    b, s, d = x.shape
    tm = min(512, s)
    return pl.pallas_call(
        _outproj_kernel,
        grid=(b, s // tm),
        in_specs=[pl.BlockSpec((1, tm, FOX_WIDTH), lambda bi, i: (bi, i, 0)),
                  pl.BlockSpec((1, tm, MLSTM_WIDTH), lambda bi, i: (bi, i, 0)),
                  pl.BlockSpec((1, tm, d), lambda bi, i: (bi, i, 0)),
                  pl.BlockSpec((1, 1, d), lambda bi, i: (bi, 0, idx)),
                  pl.BlockSpec(w_out.shape, lambda bi, i: (0, 0)),
                  pl.BlockSpec((1, d), lambda bi, i: (0, 0)),
                  pl.BlockSpec((1, d), lambda bi, i: (0, 0))],
        out_specs=pl.BlockSpec((1, tm, d), lambda bi, i: (bi, i, 0)),
        out_shape=jax.ShapeDtypeStruct((b, s, d), F32),
        compiler_params=_cparams(("parallel", "parallel")),
        name="out_proj",
    )(y_fox, y_mlstm, x, ada3, w_out, ln_g, ln_b)


def _layer(x, c, w_ada, b_ada, ffn1_w_in, ffn1_w_out, ln1_g, ln1_b, w_in, fox_f_bias, mlstm_conv_w,
           mlstm_conv_b, mlstm_i_bias, mlstm_f_bias, mlstm_norm_g, w_out, ln2_g, ln2_b,
           ffn2_w_in, ffn2_w_out, ln3_g, ln3_b):
    b, s, d = x.shape
    row = lambda a: a.reshape(1, -1)

    c_pad = jnp.zeros((8, d), F32).at[:b].set(c)
    ada3 = _ada(c_pad, w_ada, row(b_ada)).reshape(8, 1, N_ADA * d)

    x = _ffn(x, ada3, 0, ffn1_w_in.astype(BF16), ffn1_w_out.astype(BF16), row(ln1_g), row(ln1_b))

    w_mix = jnp.concatenate([w_in[:, :COL_FOX_F], w_in[:, COL_MLSTM_Q:COL_MLSTM_I], w_in[:, COL_MLSTM_O:]],
                            axis=1).astype(BF16)
    w_gate = jnp.concatenate([w_in[:, COL_FOX_F:COL_MLSTM_Q], w_in[:, COL_MLSTM_I:COL_MLSTM_O],
                              jnp.zeros((d, GATE_LANES - G_ROWS), F32)], axis=1).astype(BF16)
    gate_bias = jnp.concatenate([fox_f_bias, mlstm_i_bias, mlstm_f_bias,
                                 jnp.zeros((GATE_LANES - G_ROWS,), F32)]).reshape(1, GATE_LANES)
    z, zg = _inproj(x, ada3, 3, w_mix, w_gate)
    gates_n, gates_t, kx = _gates(zg, gate_bias, _fox_tile(s))
    y_fox = _fox(z, kx)
    y_mlstm = _mlstm(z, gates_n, gates_t, mlstm_conv_w, row(mlstm_conv_b), row(mlstm_norm_g))
    x = _outproj(y_fox, y_mlstm, x, ada3, 5, w_out.astype(BF16), row(ln2_g), row(ln2_b))

    x = _ffn(x, ada3, 6, ffn2_w_in.astype(BF16), ffn2_w_out.astype(BF16), row(ln3_g), row(ln3_b))
    return x


def kernel(x, c, w_ada, b_ada, ffn1_w_in, ffn1_w_out, ln1_g, ln1_b, w_in, fox_f_bias, mlstm_conv_w,
           mlstm_conv_b, mlstm_i_bias, mlstm_f_bias, mlstm_norm_g, w_out, ln2_g, ln2_b,
           ffn2_w_in, ffn2_w_out, ln3_g, ln3_b):
    for l in range(DEPTH):
        x = _layer(x, c, w_ada[l], b_ada[l], ffn1_w_in[l], ffn1_w_out[l], ln1_g[l], ln1_b[l],
                   w_in[l], fox_f_bias[l], mlstm_conv_w[l], mlstm_conv_b[l], mlstm_i_bias[l],
                   mlstm_f_bias[l], mlstm_norm_g[l], w_out[l], ln2_g[l], ln2_b[l],
                   ffn2_w_in[l], ffn2_w_out[l], ln3_g[l], ln3_b[l])
    return x
```

```python
import functools

import jax
import jax.numpy as jnp
from jax import lax
from jax.experimental import pallas as pl
from jax.experimental.pallas import tpu as pltpu

F32 = jnp.float32
BF16 = jnp.bfloat16

D_MODEL = 2048
DEPTH = 1
N_FOX_HEADS = 8
FOX_HEAD_DIM = 128
FOX_WIDTH = N_FOX_HEADS * FOX_HEAD_DIM
N_MLSTM_HEADS = 4
MLSTM_V_DIM = 256
MLSTM_QK_DIM = 128
MLSTM_WIDTH = N_MLSTM_HEADS * MLSTM_V_DIM
MLSTM_QK_WIDTH = N_MLSTM_HEADS * MLSTM_QK_DIM
CONV_WIDTH = 4
D_FF = 5632
N_ADA = 9
ALPHA = (2 * DEPTH) ** 0.25
LN_EPS = 1e-5

COL_FOX_F = 3 * FOX_WIDTH
COL_MLSTM_Q = COL_FOX_F + N_FOX_HEADS
COL_MLSTM_I = COL_MLSTM_Q + 2 * MLSTM_QK_WIDTH + MLSTM_WIDTH
COL_MLSTM_O = COL_MLSTM_I + 2 * N_MLSTM_HEADS
IN_WIDTH = COL_MLSTM_O + MLSTM_WIDTH

GATE_LANES = 128
G_FOX = 0
G_MI = N_FOX_HEADS
G_MF = G_MI + N_MLSTM_HEADS
G_ROWS = 16
MIX_COLS = 3 * FOX_WIDTH + 2 * MLSTM_QK_WIDTH + 2 * MLSTM_WIDTH

MLSTM_CHUNK = 256
GATE_TILE = 1024
NEG_BIG = -1e30
LOG2E = 1.4426950408889634
FOX_AUG = 3
FOX_Q_TILE = 1024
FOX_K_TILE = 512
FOX_PAIRS_PER_TRIP = 4
VMEM_LIMIT = 60 * 1024 * 1024


def _cparams(sem):
    return pltpu.CompilerParams(dimension_semantics=sem, vmem_limit_bytes=VMEM_LIMIT)


def _ln(x):
    mu = jnp.mean(x, axis=-1, keepdims=True)
    xc = x - mu
    var = jnp.mean(xc * xc, axis=-1, keepdims=True)
    return xc * lax.rsqrt(var + LN_EPS)


def _sigmoid(x):
    return 0.5 * jnp.tanh(0.5 * x) + 0.5


def _log_sigmoid(x):
    return jnp.minimum(x, 0.0) - jnp.log1p(jnp.exp(-jnp.abs(x)))


def _ada_kernel(c_ref, w_ref, b_ref, o_ref):
    c = c_ref[...]
    s = (c * _sigmoid(c)).astype(BF16)
    o_ref[...] = jnp.dot(s, w_ref[...].astype(BF16), preferred_element_type=F32) + b_ref[...]


def _ada(c_pad, w_ada, b_ada):
    rows, d = c_pad.shape
    n = w_ada.shape[1]
    tn = 1024
    return pl.pallas_call(
        _ada_kernel,
        grid=(n // tn,),
        in_specs=[pl.BlockSpec((rows, d), lambda j: (0, 0)),
                  pl.BlockSpec((d, tn), lambda j: (0, j)),
                  pl.BlockSpec((1, tn), lambda j: (0, j))],
        out_specs=pl.BlockSpec((rows, tn), lambda j: (0, j)),
        out_shape=jax.ShapeDtypeStruct((rows, n), F32),
        compiler_params=_cparams(("arbitrary",)),
        name="ada_proj",
    )(c_pad, w_ada, b_ada)


def _ffn_kernel(x_ref, sh_ref, sc_ref, gt_ref, wg_ref, wu_ref, wo_ref, lng_ref, lnb_ref, o_ref, h_ref):
    k = pl.program_id(2)
    last = pl.num_programs(2) - 1

    def swiglu_part(h):
        g = jnp.dot(h, wg_ref[...], preferred_element_type=F32)
        u = jnp.dot(h, wu_ref[...], preferred_element_type=F32)
        a = (g * _sigmoid(g) * u).astype(BF16)
        return jnp.dot(a, wo_ref[...], preferred_element_type=F32)

    halves = [pl.ds(r * (x_ref.shape[1] // 2), x_ref.shape[1] // 2) for r in range(2)]

    @pl.when(k == 0)
    def _():
        for rows in halves:
            h = (_ln(x_ref[0, rows, :]) * (1.0 + sc_ref[0]) + sh_ref[0]).astype(BF16)
            h_ref[rows, :] = h
            o_ref[0, rows, :] = swiglu_part(h)

    @pl.when((k > 0) & (k < last))
    def _():
        o_ref[0] += swiglu_part(h_ref[...])

    @pl.when(k == last)
    def _():
        for rows in halves:
            y = (ALPHA * x_ref[0, rows, :]
                 + (0.5 * (1.0 + gt_ref[0])) * (o_ref[0, rows, :] + swiglu_part(h_ref[rows, :])))
            o_ref[0, rows, :] = _ln(y) * lng_ref[...] + lnb_ref[...]


def _ffn(x, ada3, idx, w_in, w_out, ln_g, ln_b):
    b, s, d = x.shape
    dff = w_out.shape[0]
    tm = min(1024, s)
    tf = 512
    nk = dff // tf
    vec = lambda j: pl.BlockSpec((1, 1, d), lambda bi, i, k: (bi, 0, j))
    return pl.pallas_call(
        _ffn_kernel,
        grid=(b, s // tm, nk),
        in_specs=[pl.BlockSpec((1, tm, d), lambda bi, i, k: (bi, i, 0)),
                  vec(idx), vec(idx + 1), vec(idx + 2),
                  pl.BlockSpec((d, tf), lambda bi, i, k: (0, k)),
                  pl.BlockSpec((d, tf), lambda bi, i, k: (0, k + nk)),
                  pl.BlockSpec((tf, d), lambda bi, i, k: (k, 0)),
                  pl.BlockSpec((1, d), lambda bi, i, k: (0, 0)),
                  pl.BlockSpec((1, d), lambda bi, i, k: (0, 0))],
        out_specs=pl.BlockSpec((1, tm, d), lambda bi, i, k: (bi, i, 0)),
        out_shape=jax.ShapeDtypeStruct((b, s, d), F32),
        scratch_shapes=[pltpu.VMEM((tm, d), BF16)],
        compiler_params=_cparams(("parallel", "parallel", "arbitrary")),
        name="ffn",
    )(x, ada3, ada3, ada3, w_in, w_in, w_out, ln_g, ln_b)


def _inproj_kernel(x_ref, sh_ref, sc_ref, w_ref, cs_ref, wgate_ref, z_ref, zg_ref, h_ref):
    j = pl.program_id(2)

    def project(h):
        z = jnp.dot(h, w_ref[...], preferred_element_type=F32)
        z_ref[0] = (z * cs_ref[...]).astype(BF16)

    @pl.when(j == 0)
    def _():
        h = (_ln(x_ref[0]) * (1.0 + sc_ref[0]) + sh_ref[0]).astype(BF16)
        h_ref[...] = h
        zg_ref[0] = jnp.dot(h, wgate_ref[...], preferred_element_type=F32)
        project(h)

    @pl.when(j > 0)
    def _():
        project(h_ref[...])


def _inproj(x, ada3, idx, w_mix, w_gate):
    b, s, d = x.shape
    n = w_mix.shape[1]
    tm = min(1024, s)
    tn = 2048
    vec = lambda j: pl.BlockSpec((1, 1, d), lambda bi, i, jj: (bi, 0, j))
    col_scale = jnp.where(jnp.arange(n) < FOX_WIDTH, FOX_HEAD_DIM ** -0.5 * LOG2E, 1.0).astype(F32).reshape(1, n)
    return pl.pallas_call(
        _inproj_kernel,
        grid=(b, s // tm, n // tn),
        in_specs=[pl.BlockSpec((1, tm, d), lambda bi, i, j: (bi, i, 0)),
                  vec(idx), vec(idx + 1),
                  pl.BlockSpec((d, tn), lambda bi, i, j: (0, j)),
                  pl.BlockSpec((1, tn), lambda bi, i, j: (0, j)),
                  pl.BlockSpec((d, GATE_LANES), lambda bi, i, j: (0, 0))],
        out_specs=[pl.BlockSpec((1, tm, tn), lambda bi, i, j: (bi, i, j)),
                   pl.BlockSpec((1, tm, GATE_LANES), lambda bi, i, j: (bi, i, 0))],
        out_shape=[jax.ShapeDtypeStruct((b, s, n), BF16),
                   jax.ShapeDtypeStruct((b, s, GATE_LANES), F32)],
        scratch_shapes=[pltpu.VMEM((tm, d), BF16)],
        compiler_params=_cparams(("parallel", "parallel", "arbitrary")),
        name="in_proj",
    )(x, ada3, ada3, w_mix, col_scale, w_gate)


def _split3(v):
    hi = v.astype(BF16)
    r = v - hi.astype(F32)
    mid = r.astype(BF16)
    lo = (r - mid.astype(F32)).astype(BF16)
    return hi, mid, lo


def _gate_kernel(zg_ref, bias_ref, nat_ref, t_ref, kx_ref, carry_ref, *, chunk):
    i = pl.program_id(1)

    @pl.when(i == 0)
    def _():
        carry_ref[...] = jnp.zeros_like(carry_ref)

    rows = chunk
    col = lax.broadcasted_iota(jnp.int32, (rows, GATE_LANES), 1)
    is_in_gate = (col >= G_MI) & (col < G_MF)
    r_i = lax.broadcasted_iota(jnp.int32, (rows, rows), 0)
    c_i = lax.broadcasted_iota(jnp.int32, (rows, rows), 1)
    tri = jnp.where(r_i >= c_i, 1.0, 0.0).astype(BF16)
    for c in range(zg_ref.shape[1] // rows):
        span = pl.ds(c * rows, rows)
        z = zg_ref[0, span, :] + bias_ref[...]
        v = jnp.where(is_in_gate, z, _log_sigmoid(z))
        hi, mid, lo = _split3(v)
        csum = (jnp.dot(tri, hi, preferred_element_type=F32)
                + jnp.dot(tri, mid, preferred_element_type=F32)
                + jnp.dot(tri, lo, preferred_element_type=F32))
        running = csum + carry_ref[...]
        carry_ref[...] = running[rows - 1:rows, :]
        out = jnp.where(col < G_MI, running, jnp.where(is_in_gate, v, csum))
        nat_ref[0, span, :] = out
        t_ref[0, :, span] = out.T[:G_ROWS, :]
        for h in range(N_FOX_HEADS):
            parts = _split3(jnp.broadcast_to(running[:, h:h + 1] * (-LOG2E), z.shape))
            tile = jnp.zeros(z.shape, F32)
            for t in reversed(range(FOX_AUG)):
                tile = jnp.where(col == t, parts[t].astype(F32), tile)
            kx_ref[0, span, h * GATE_LANES:(h + 1) * GATE_LANES] = tile.astype(BF16)


def _gates(zg, bias):
    b, s, _ = zg.shape
    chunk = min(MLSTM_CHUNK, s)
    tl = min(GATE_TILE, s)
    return pl.pallas_call(
        functools.partial(_gate_kernel, chunk=chunk),
        grid=(b, s // tl),
        in_specs=[pl.BlockSpec((1, tl, GATE_LANES), lambda bi, i: (bi, i, 0)),
                  pl.BlockSpec((1, GATE_LANES), lambda bi, i: (0, 0))],
        out_specs=[pl.BlockSpec((1, tl, GATE_LANES), lambda bi, i: (bi, i, 0)),
                   pl.BlockSpec((1, G_ROWS, tl), lambda bi, i: (bi, 0, i)),
                   pl.BlockSpec((1, tl, N_FOX_HEADS * GATE_LANES), lambda bi, i: (bi, i, 0))],
        out_shape=[jax.ShapeDtypeStruct((b, s, GATE_LANES), F32),
                   jax.ShapeDtypeStruct((b, G_ROWS, s), F32),
                   jax.ShapeDtypeStruct((b, s, N_FOX_HEADS * GATE_LANES), BF16)],
        scratch_shapes=[pltpu.VMEM((1, GATE_LANES), F32)],
        compiler_params=_cparams(("parallel", "arbitrary")),
        name="gates",
    )(zg, bias)


def _fox_kernel(q_ref, k_ref, kx_ref, v_ref, o_ref, s_buf, p_buf, a_buf, m_ref, acc_ref, *, tq, tk):
    qi = pl.program_id(2)
    dh = FOX_HEAD_DIM
    n_diag = tq // tk
    assert n_diag % 2 == 0
    n = qi * n_diag
    lane_q = lax.broadcasted_iota(jnp.int32, (tq, dh), 1)
    lane_k = lax.broadcasted_iota(jnp.int32, (tk, dh), 1)
    q = jnp.concatenate([q_ref[0], jnp.where(lane_q < FOX_AUG, 1.0, 0.0).astype(BF16)], axis=1)
    v_ones = jnp.where(lane_k == 0, 1.0, 0.0).astype(BF16)

    def logits(slot, j):
        start = pl.multiple_of(j * tk, tk)
        k = jnp.concatenate([k_ref[0, pl.ds(start, tk), :], kx_ref[0, pl.ds(start, tk), :]], axis=1)
        s_buf[slot] = lax.dot_general(q, k, (((1,), (1,)), ((), ())), preferred_element_type=F32)

    def softmax(slot, diag):
        s = s_buf[slot]
        if diag is not None:
            r_i = lax.broadcasted_iota(jnp.int32, s.shape, 0)
            c_i = lax.broadcasted_iota(jnp.int32, s.shape, 1)
            s = jnp.where(c_i + diag * tk <= r_i, s, NEG_BIG)
        m = m_ref[...]
        m_new = jnp.maximum(m, jnp.broadcast_to(jnp.max(s, axis=-1, keepdims=True), m.shape))
        for c in range(tk // dh):
            p_buf[slot, :, c * dh:(c + 1) * dh] = jnp.exp2(s[:, c * dh:(c + 1) * dh] - m_new).astype(BF16)
        a_buf[slot] = jnp.exp2(m - m_new)
        m_ref[...] = m_new

    def accumulate(slot, j):
        start = pl.multiple_of(j * tk, tk)
        v = jnp.concatenate([v_ref[0, pl.ds(start, tk), :], v_ones], axis=1)
        pv = jnp.dot(p_buf[slot], v, preferred_element_type=F32)
        a = a_buf[slot]
        for c in range(2):
            acc_ref[:, c * dh:(c + 1) * dh] = a * acc_ref[:, c * dh:(c + 1) * dh] + pv[:, c * dh:(c + 1) * dh]

    def block_of(t):
        if isinstance(t, int):
            return n + t if t < n_diag else t - n_diag
        return jnp.where(t < n_diag, n + t, t - n_diag)

    m_ref[...] = jnp.full(m_ref.shape, NEG_BIG, F32)
    acc_ref[...] = jnp.zeros(acc_ref.shape, F32)
    logits(0, block_of(0))
    for t in range(n_diag):
        logits(1 - t % 2, block_of(t + 1))
        softmax(t % 2, t)
        if t > 0:
            accumulate(1 - t % 2, block_of(t - 1))

    def pair(i):
        t = 2 * i
        logits(1, block_of(t + 1))
        softmax(0, None)
        accumulate(1, block_of(t - 1))
        logits(0, block_of(t + 2))
        softmax(1, None)
        accumulate(0, block_of(t))

    first, n_pairs = n_diag // 2, n // 2

    def trip(g, carry):
        for u in range(FOX_PAIRS_PER_TRIP):
            pair(first + FOX_PAIRS_PER_TRIP * g + u)
        return carry

    n_trips = n_pairs // FOX_PAIRS_PER_TRIP
    lax.fori_loop(0, n_trips, trip, 0)
    done = n_trips * FOX_PAIRS_PER_TRIP
    size = FOX_PAIRS_PER_TRIP // 2
    while size >= 1:
        @pl.when((n_pairs & size) != 0)
        def _(done=done, size=size):
            for u in range(size):
                pair(first + done + u)
        done = done + (n_pairs & size)
        size //= 2

    n_steps = n + n_diag
    accumulate(1, block_of(n_steps - 1))
    acc = acc_ref[...]
    o_ref[0] = (acc[:, :dh] / acc[:, dh:dh + 1]).astype(o_ref.dtype)


def _fox(z, kx):
    b, s, _ = z.shape
    tq = min(FOX_Q_TILE, s)
    tk = min(FOX_K_TILE, tq // 2)
    h = N_FOX_HEADS
    return pl.pallas_call(
        functools.partial(_fox_kernel, tq=tq, tk=tk),
        grid=(b, h, s // tq),
        in_specs=[pl.BlockSpec((1, tq, FOX_HEAD_DIM), lambda bi, hi, qi: (bi, qi, hi)),
                  pl.BlockSpec((1, s, FOX_HEAD_DIM), lambda bi, hi, qi: (bi, 0, h + hi)),
                  pl.BlockSpec((1, s, GATE_LANES), lambda bi, hi, qi: (bi, 0, hi)),
                  pl.BlockSpec((1, s, FOX_HEAD_DIM), lambda bi, hi, qi: (bi, 0, 2 * h + hi))],
        out_specs=pl.BlockSpec((1, tq, FOX_HEAD_DIM), lambda bi, hi, qi: (bi, qi, hi)),
        out_shape=jax.ShapeDtypeStruct((b, s, FOX_WIDTH), BF16),
        scratch_shapes=[pltpu.VMEM((2, tq, tk), F32),
                        pltpu.VMEM((2, tq, tk), BF16),
                        pltpu.VMEM((2, tq, FOX_HEAD_DIM), F32),
                        pltpu.VMEM((tq, FOX_HEAD_DIM), F32),
                        pltpu.VMEM((tq, 2 * FOX_HEAD_DIM), F32)],
        compiler_params=_cparams(("parallel", "parallel", "arbitrary")),
        name="fox_attn",
    )(z, z, kx, z)


def _mlstm_kernel(qk_ref, v_ref, o_ref, gn_ref, gt_ref, cw_ref, cb_ref, ng_ref, y_ref,
                  c_st, n_st, m_st, tail, ubuf):
    c = pl.program_id(0)
    L = qk_ref.shape[1]

    @pl.when(c == 0)
    def _():
        c_st[...] = jnp.zeros_like(c_st)
        n_st[...] = jnp.zeros_like(n_st)
        m_st[...] = jnp.zeros_like(m_st)
        tail[...] = jnp.zeros_like(tail)

    r_i = lax.broadcasted_iota(jnp.int32, (L, L), 0)
    c_i = lax.broadcasted_iota(jnp.int32, (L, L), 1)
    causal = c_i <= r_i
    for bi in range(qk_ref.shape[0]):
        _mlstm_chunk(bi, causal, qk_ref, v_ref, o_ref, gn_ref, gt_ref, cw_ref, cb_ref, ng_ref, y_ref,
                     c_st, n_st, m_st, tail, ubuf)


def _mlstm_chunk(bi, causal, qk_ref, v_ref, o_ref, gn_ref, gt_ref, cw_ref, cb_ref, ng_ref, y_ref,
                 c_st, n_st, m_st, tail, ubuf):
    L = qk_ref.shape[1]
    dk, dv = MLSTM_QK_DIM, MLSTM_V_DIM
    u = qk_ref[bi].astype(F32)
    ubuf[bi, 0:8, :] = tail[bi]
    ubuf[bi, 8:8 + L, :] = u
    tail[bi] = u[L - 8:L, :]
    conv = cb_ref[...] + cw_ref[3:4, :] * u
    for d in range(1, CONV_WIDTH):
        conv = conv + cw_ref[3 - d:4 - d, :] * ubuf[bi, 8 - d:8 - d + L, :]
    qk = conv * _sigmoid(conv)

    gn = gn_ref[bi]
    gt = gt_ref[bi]

    for h in range(N_MLSTM_HEADS):
        st = bi * N_MLSTM_HEADS + h
        qf = qk[:, h * dk:(h + 1) * dk]
        q = qf.astype(BF16)
        kf = qk[:, MLSTM_QK_WIDTH + h * dk:MLSTM_QK_WIDTH + (h + 1) * dk] * (dk ** -0.5)
        v = v_ref[bi, :, h * dv:(h + 1) * dv]
        bcol = gn[:, G_MF + h:G_MF + h + 1]
        icol = gn[:, G_MI + h:G_MI + h + 1]
        brow = gt[G_MF + h:G_MF + h + 1, :]
        irow = gt[G_MI + h:G_MI + h + 1, :]
        m_prev = m_st[st][:, 0:1]
        c_prev = c_st[st]
        n_prev = n_st[st]

        m_inter = bcol + m_prev
        dlog = jnp.where(causal, bcol - brow + irow, NEG_BIG)
        m_t = jnp.maximum(m_inter, jnp.max(dlog, axis=-1, keepdims=True))
        s = lax.dot_general(q, kf.astype(BF16), (((1,), (1,)), ((), ())), preferred_element_type=F32)
        s = s * jnp.exp(dlog - m_t)
        inter = jnp.exp(m_inter - m_t)
        num = (inter * jnp.dot(q, c_prev.astype(BF16), preferred_element_type=F32)
               + jnp.dot(s.astype(BF16), v, preferred_element_type=F32))
        den = inter * jnp.sum(qf * n_prev, axis=-1, keepdims=True) + jnp.sum(s, axis=-1, keepdims=True)
        hh = num / jnp.maximum(jnp.abs(den), jnp.exp(-m_t))

        b_last = bcol[L - 1:L, :]
        wlog = b_last - bcol + icol
        m_new = jnp.maximum(b_last + m_prev, jnp.max(wlog, axis=0, keepdims=True))
        decay = jnp.exp(b_last + m_prev - m_new)
        wk = kf * jnp.exp(wlog - m_new)
        c_st[st] = decay * c_prev + jnp.dot(wk.T.astype(BF16), v, preferred_element_type=F32)
        n_st[st] = decay * n_prev + jnp.sum(wk, axis=0, keepdims=True)
        m_st[st] = jnp.broadcast_to(m_new, (1, GATE_LANES))

        hn = hh * lax.rsqrt(jnp.mean(hh * hh, axis=-1, keepdims=True) + LN_EPS)
        hn = hn * ng_ref[:, h * dv:(h + 1) * dv]
        og = _sigmoid(o_ref[bi, :, h * dv:(h + 1) * dv].astype(F32))
        y_ref[bi, :, h * dv:(h + 1) * dv] = (og * hn).astype(y_ref.dtype)


def _mlstm(z, gates_n, gates_t, conv_w, conv_b, norm_g):
    b, s, _ = z.shape
    L = min(MLSTM_CHUNK, s)
    wq = 2 * MLSTM_QK_WIDTH
    assert wq == MLSTM_WIDTH == FOX_WIDTH
    base = 3 * FOX_WIDTH // wq
    return pl.pallas_call(
        _mlstm_kernel,
        grid=(s // L,),
        in_specs=[pl.BlockSpec((b, L, wq), lambda ci: (0, ci, base)),
                  pl.BlockSpec((b, L, MLSTM_WIDTH), lambda ci: (0, ci, base + 1)),
                  pl.BlockSpec((b, L, MLSTM_WIDTH), lambda ci: (0, ci, base + 2)),
                  pl.BlockSpec((b, L, GATE_LANES), lambda ci: (0, ci, 0)),
                  pl.BlockSpec((b, G_ROWS, L), lambda ci: (0, 0, ci)),
                  pl.BlockSpec((CONV_WIDTH, wq), lambda ci: (0, 0)),
                  pl.BlockSpec((1, wq), lambda ci: (0, 0)),
                  pl.BlockSpec((1, MLSTM_WIDTH), lambda ci: (0, 0))],
        out_specs=pl.BlockSpec((b, L, MLSTM_WIDTH), lambda ci: (0, ci, 0)),
        out_shape=jax.ShapeDtypeStruct((b, s, MLSTM_WIDTH), BF16),
        scratch_shapes=[pltpu.VMEM((b * N_MLSTM_HEADS, MLSTM_QK_DIM, MLSTM_V_DIM), F32),
                        pltpu.VMEM((b * N_MLSTM_HEADS, 1, MLSTM_QK_DIM), F32),
                        pltpu.VMEM((b * N_MLSTM_HEADS, 1, GATE_LANES), F32),
                        pltpu.VMEM((b, 8, wq), F32),
                        pltpu.VMEM((b, L + 8, wq), F32)],
        compiler_params=_cparams(("arbitrary",)),
        name="mlstm",
    )(z, z, z, gates_n, gates_t, conv_w, conv_b, norm_g)


def _outproj_kernel(yf_ref, ym_ref, x_ref, gt_ref, w_ref, lng_ref, lnb_ref, o_ref):
    nf = yf_ref.shape[2]
    half = x_ref.shape[1] // 2
    for r in range(2):
        rows = pl.ds(r * half, half)
        hmix = (jnp.dot(yf_ref[0, rows, :], w_ref[0:nf, :], preferred_element_type=F32)
                + jnp.dot(ym_ref[0, rows, :], w_ref[nf:, :], preferred_element_type=F32))
        y = ALPHA * x_ref[0, rows, :] + (1.0 + gt_ref[0]) * hmix
        o_ref[0, rows, :] = _ln(y) * lng_ref[...] + lnb_ref[...]


def _outproj(y_fox, y_mlstm, x, ada3, idx, w_out, ln_g, ln_b):
    b, s, d = x.shape
    tm = min(512, s)
    return pl.pallas_call(
        _outproj_kernel,
        grid=(b, s // tm),
        in_specs=[pl.BlockSpec((1, tm, FOX_WIDTH), lambda bi, i: (bi, i, 0)),
                  pl.BlockSpec((1, tm, MLSTM_WIDTH), lambda bi, i: (bi, i, 0)),
                  pl.BlockSpec((1, tm, d), lambda bi, i: (bi, i, 0)),
                  pl.BlockSpec((1, 1, d), lambda bi, i: (bi, 0, idx)),
                  pl.BlockSpec(w_out.shape, lambda bi, i: (0, 0)),
                  pl.BlockSpec((1, d), lambda bi, i: (0, 0)),
                  pl.BlockSpec((1, d), lambda bi, i: (0, 0))],
        out_specs=pl.BlockSpec((1, tm, d), lambda bi, i: (bi, i, 0)),
        out_shape=jax.ShapeDtypeStruct((b, s, d), F32),
        compiler_params=_cparams(("parallel", "parallel")),
        name="out_proj",
    )(y_fox, y_mlstm, x, ada3, w_out, ln_g, ln_b)


def _layer(x, c, w_ada, b_ada, ffn1_w_in, ffn1_w_out, ln1_g, ln1_b, w_in, fox_f_bias, mlstm_conv_w,
           mlstm_conv_b, mlstm_i_bias, mlstm_f_bias, mlstm_norm_g, w_out, ln2_g, ln2_b,
           ffn2_w_in, ffn2_w_out, ln3_g, ln3_b):
    b, s, d = x.shape
    row = lambda a: a.reshape(1, -1)

    c_pad = jnp.zeros((8, d), F32).at[:b].set(c)
    ada3 = _ada(c_pad, w_ada, row(b_ada)).reshape(8, 1, N_ADA * d)

    x = _ffn(x, ada3, 0, ffn1_w_in.astype(BF16), ffn1_w_out.astype(BF16), row(ln1_g), row(ln1_b))

    w_in = w_in.astype(BF16)
    w_mix = jnp.concatenate([w_in[:, :COL_FOX_F], w_in[:, COL_MLSTM_Q:COL_MLSTM_I], w_in[:, COL_MLSTM_O:]], axis=1)
    w_gate = jnp.concatenate([w_in[:, COL_FOX_F:COL_MLSTM_Q], w_in[:, COL_MLSTM_I:COL_MLSTM_O],
                              jnp.zeros((d, GATE_LANES - G_ROWS), BF16)], axis=1)
    gate_bias = jnp.concatenate([fox_f_bias, mlstm_i_bias, mlstm_f_bias,
                                 jnp.zeros((GATE_LANES - G_ROWS,), F32)]).reshape(1, GATE_LANES)
    z, zg = _inproj(x, ada3, 3, w_mix, w_gate)
    gates_n, gates_t, kx = _gates(zg, gate_bias)
    y_fox = _fox(z, kx)
    y_mlstm = _mlstm(z, gates_n, gates_t, mlstm_conv_w, row(mlstm_conv_b), row(mlstm_norm_g))
    x = _outproj(y_fox, y_mlstm, x, ada3, 5, w_out.astype(BF16), row(ln2_g), row(ln2_b))

    x = _ffn(x, ada3, 6, ffn2_w_in.astype(BF16), ffn2_w_out.astype(BF16), row(ln3_g), row(ln3_b))
    return x


def kernel(x, c, w_ada, b_ada, ffn1_w_in, ffn1_w_out, ln1_g, ln1_b, w_in, fox_f_bias, mlstm_conv_w,
           mlstm_conv_b, mlstm_i_bias, mlstm_f_bias, mlstm_norm_g, w_out, ln2_g, ln2_b,
           ffn2_w_in, ffn2_w_out, ln3_g, ln3_b):
    for l in range(DEPTH):
        x = _layer(x, c, w_ada[l], b_ada[l], ffn1_w_in[l], ffn1_w_out[l], ln1_g[l], ln1_b[l],
                   w_in[l], fox_f_bias[l], mlstm_conv_w[l], mlstm_conv_b[l], mlstm_i_bias[l],
                   mlstm_f_bias[l], mlstm_norm_g[l], w_out[l], ln2_g[l], ln2_b[l],
                   ffn2_w_in[l], ffn2_w_out[l], ln3_g[l], ln3_b[l])
    return x
```

```python
import functools

import jax
import jax.numpy as jnp
from jax import lax
from jax.experimental import pallas as pl
from jax.experimental.pallas import tpu as pltpu

F32 = jnp.float32
BF16 = jnp.bfloat16

D_MODEL = 2048
DEPTH = 1
N_FOX_HEADS = 8
FOX_HEAD_DIM = 128
FOX_WIDTH = N_FOX_HEADS * FOX_HEAD_DIM
N_MLSTM_HEADS = 4
MLSTM_V_DIM = 256
MLSTM_QK_DIM = 128
MLSTM_WIDTH = N_MLSTM_HEADS * MLSTM_V_DIM
MLSTM_QK_WIDTH = N_MLSTM_HEADS * MLSTM_QK_DIM
CONV_WIDTH = 4
D_FF = 5632
N_ADA = 9
ALPHA = (2 * DEPTH) ** 0.25
LN_EPS = 1e-5

COL_FOX_F = 3 * FOX_WIDTH
COL_MLSTM_Q = COL_FOX_F + N_FOX_HEADS
COL_MLSTM_I = COL_MLSTM_Q + 2 * MLSTM_QK_WIDTH + MLSTM_WIDTH
COL_MLSTM_O = COL_MLSTM_I + 2 * N_MLSTM_HEADS
IN_WIDTH = COL_MLSTM_O + MLSTM_WIDTH

GATE_LANES = 128
G_FOX = 0
G_MI = N_FOX_HEADS
G_MF = G_MI + N_MLSTM_HEADS
G_ROWS = 16
MIX_COLS = 3 * FOX_WIDTH + 2 * MLSTM_QK_WIDTH + 2 * MLSTM_WIDTH

MLSTM_CHUNK = 256
GATE_TILE = 1024
NEG_BIG = -1e30
LOG2E = 1.4426950408889634
FOX_AUG = 3
FOX_Q_TILE = 1024
FOX_K_TILE = 512
FOX_PAIRS_PER_TRIP = 4
VMEM_LIMIT = 60 * 1024 * 1024


def _cparams(sem):
    return pltpu.CompilerParams(dimension_semantics=sem, vmem_limit_bytes=VMEM_LIMIT)


def _ln(x):
    mu = jnp.mean(x, axis=-1, keepdims=True)
    xc = x - mu
    var = jnp.mean(xc * xc, axis=-1, keepdims=True)
    return xc * lax.rsqrt(var + LN_EPS)


def _sigmoid(x):
    return 0.5 * jnp.tanh(0.5 * x) + 0.5


def _log_sigmoid(x):
    return jnp.minimum(x, 0.0) - jnp.log1p(jnp.exp(-jnp.abs(x)))


def _ada_kernel(c_ref, w_ref, b_ref, o_ref):
    c = c_ref[...]
    s = (c * _sigmoid(c)).astype(BF16)
    o_ref[...] = jnp.dot(s, w_ref[...].astype(BF16), preferred_element_type=F32) + b_ref[...]


def _ada(c_pad, w_ada, b_ada):
    rows, d = c_pad.shape
    n = w_ada.shape[1]
    tn = 1024
    return pl.pallas_call(
        _ada_kernel,
        grid=(n // tn,),
        in_specs=[pl.BlockSpec((rows, d), lambda j: (0, 0)),
                  pl.BlockSpec((d, tn), lambda j: (0, j)),
                  pl.BlockSpec((1, tn), lambda j: (0, j))],
        out_specs=pl.BlockSpec((rows, tn), lambda j: (0, j)),
        out_shape=jax.ShapeDtypeStruct((rows, n), F32),
        compiler_params=_cparams(("arbitrary",)),
        name="ada_proj",
    )(c_pad, w_ada, b_ada)


def _ffn_kernel(x_ref, sh_ref, sc_ref, gt_ref, wg_ref, wu_ref, wo_ref, lng_ref, lnb_ref, o_ref, h_ref):
    k = pl.program_id(2)
    last = pl.num_programs(2) - 1

    def swiglu_part(h):
        g = jnp.dot(h, wg_ref[...], preferred_element_type=F32)
        u = jnp.dot(h, wu_ref[...], preferred_element_type=F32)
        a = (g * _sigmoid(g) * u).astype(BF16)
        return jnp.dot(a, wo_ref[...], preferred_element_type=F32)

    halves = [pl.ds(r * (x_ref.shape[1] // 2), x_ref.shape[1] // 2) for r in range(2)]

    @pl.when(k == 0)
    def _():
        for rows in halves:
            h = (_ln(x_ref[0, rows, :]) * (1.0 + sc_ref[0]) + sh_ref[0]).astype(BF16)
            h_ref[rows, :] = h
            o_ref[0, rows, :] = swiglu_part(h)

    @pl.when((k > 0) & (k < last))
    def _():
        o_ref[0] += swiglu_part(h_ref[...])

    @pl.when(k == last)
    def _():
        for rows in halves:
            y = (ALPHA * x_ref[0, rows, :]
                 + (0.5 * (1.0 + gt_ref[0])) * (o_ref[0, rows, :] + swiglu_part(h_ref[rows, :])))
            o_ref[0, rows, :] = _ln(y) * lng_ref[...] + lnb_ref[...]


def _ffn(x, ada3, idx, w_in, w_out, ln_g, ln_b):
    b, s, d = x.shape
    dff = w_out.shape[0]
    tm = min(1024, s)
    tf = 512
    nk = dff // tf
    vec = lambda j: pl.BlockSpec((1, 1, d), lambda bi, i, k: (bi, 0, j))
    return pl.pallas_call(
        _ffn_kernel,
        grid=(b, s // tm, nk),
        in_specs=[pl.BlockSpec((1, tm, d), lambda bi, i, k: (bi, i, 0)),
                  vec(idx), vec(idx + 1), vec(idx + 2),
                  pl.BlockSpec((d, tf), lambda bi, i, k: (0, k)),
                  pl.BlockSpec((d, tf), lambda bi, i, k: (0, k + nk)),
                  pl.BlockSpec((tf, d), lambda bi, i, k: (k, 0)),
                  pl.BlockSpec((1, d), lambda bi, i, k: (0, 0)),
                  pl.BlockSpec((1, d), lambda bi, i, k: (0, 0))],
        out_specs=pl.BlockSpec((1, tm, d), lambda bi, i, k: (bi, i, 0)),
        out_shape=jax.ShapeDtypeStruct((b, s, d), F32),
        scratch_shapes=[pltpu.VMEM((tm, d), BF16)],
        compiler_params=_cparams(("parallel", "parallel", "arbitrary")),
        name="ffn",
    )(x, ada3, ada3, ada3, w_in, w_in, w_out, ln_g, ln_b)


def _inproj_kernel(x_ref, sh_ref, sc_ref, w_ref, cs_ref, wgate_ref, z_ref, zg_ref, h_ref):
    j = pl.program_id(2)

    def project(h):
        z = jnp.dot(h, w_ref[...], preferred_element_type=F32)
        z_ref[0] = (z * cs_ref[...]).astype(BF16)

    @pl.when(j == 0)
    def _():
        h = (_ln(x_ref[0]) * (1.0 + sc_ref[0]) + sh_ref[0]).astype(BF16)
        h_ref[...] = h
        zg_ref[0] = jnp.dot(h, wgate_ref[...], preferred_element_type=F32)
        project(h)

    @pl.when(j > 0)
    def _():
        project(h_ref[...])


def _inproj(x, ada3, idx, w_mix, w_gate):
    b, s, d = x.shape
    n = w_mix.shape[1]
    tm = min(1024, s)
    tn = 2048
    vec = lambda j: pl.BlockSpec((1, 1, d), lambda bi, i, jj: (bi, 0, j))
    col_scale = jnp.where(jnp.arange(n) < FOX_WIDTH, FOX_HEAD_DIM ** -0.5 * LOG2E, 1.0).astype(F32).reshape(1, n)
    return pl.pallas_call(
        _inproj_kernel,
        grid=(b, s // tm, n // tn),
        in_specs=[pl.BlockSpec((1, tm, d), lambda bi, i, j: (bi, i, 0)),
                  vec(idx), vec(idx + 1),
                  pl.BlockSpec((d, tn), lambda bi, i, j: (0, j)),
                  pl.BlockSpec((1, tn), lambda bi, i, j: (0, j)),
                  pl.BlockSpec((d, GATE_LANES), lambda bi, i, j: (0, 0))],
        out_specs=[pl.BlockSpec((1, tm, tn), lambda bi, i, j: (bi, i, j)),
                   pl.BlockSpec((1, tm, GATE_LANES), lambda bi, i, j: (bi, i, 0))],
        out_shape=[jax.ShapeDtypeStruct((b, s, n), BF16),
                   jax.ShapeDtypeStruct((b, s, GATE_LANES), F32)],
        scratch_shapes=[pltpu.VMEM((tm, d), BF16)],
        compiler_params=_cparams(("parallel", "parallel", "arbitrary")),
        name="in_proj",
    )(x, ada3, ada3, w_mix, col_scale, w_gate)


def _split3(v):
    hi = v.astype(BF16)
    r = v - hi.astype(F32)
    mid = r.astype(BF16)
    lo = (r - mid.astype(F32)).astype(BF16)
    return hi, mid, lo


def _gate_kernel(zg_ref, bias_ref, nat_ref, t_ref, kx_ref, carry_ref, *, chunk):
    i = pl.program_id(1)

    @pl.when(i == 0)
    def _():
        carry_ref[...] = jnp.zeros_like(carry_ref)

    rows = chunk
    col = lax.broadcasted_iota(jnp.int32, (rows, GATE_LANES), 1)
    is_in_gate = (col >= G_MI) & (col < G_MF)
    r_i = lax.broadcasted_iota(jnp.int32, (rows, rows), 0)
    c_i = lax.broadcasted_iota(jnp.int32, (rows, rows), 1)
    tri = jnp.where(r_i >= c_i, 1.0, 0.0).astype(BF16)
    for c in range(zg_ref.shape[1] // rows):
        span = pl.ds(c * rows, rows)
        z = zg_ref[0, span, :] + bias_ref[...]
        v = jnp.where(is_in_gate, z, _log_sigmoid(z))
        hi, mid, lo = _split3(v)
        csum = (jnp.dot(tri, hi, preferred_element_type=F32)
                + jnp.dot(tri, mid, preferred_element_type=F32)
                + jnp.dot(tri, lo, preferred_element_type=F32))
        running = csum + carry_ref[...]
        carry_ref[...] = running[rows - 1:rows, :]
        out = jnp.where(col < G_MI, running, jnp.where(is_in_gate, v, csum))
        nat_ref[0, span, :] = out
        t_ref[0, :, span] = out.T[:G_ROWS, :]
        for h in range(N_FOX_HEADS):
            parts = _split3(jnp.broadcast_to(running[:, h:h + 1] * (-LOG2E), z.shape))
            tile = jnp.zeros(z.shape, F32)
            for t in reversed(range(FOX_AUG)):
                tile = jnp.where(col == t, parts[t].astype(F32), tile)
            kx_ref[0, span, h * GATE_LANES:(h + 1) * GATE_LANES] = tile.astype(BF16)


def _gates(zg, bias):
    b, s, _ = zg.shape
    chunk = min(MLSTM_CHUNK, s)
    tl = min(GATE_TILE, s)
    return pl.pallas_call(
        functools.partial(_gate_kernel, chunk=chunk),
        grid=(b, s // tl),
        in_specs=[pl.BlockSpec((1, tl, GATE_LANES), lambda bi, i: (bi, i, 0)),
                  pl.BlockSpec((1, GATE_LANES), lambda bi, i: (0, 0))],
        out_specs=[pl.BlockSpec((1, tl, GATE_LANES), lambda bi, i: (bi, i, 0)),
                   pl.BlockSpec((1, G_ROWS, tl), lambda bi, i: (bi, 0, i)),
                   pl.BlockSpec((1, tl, N_FOX_HEADS * GATE_LANES), lambda bi, i: (bi, i, 0))],
        out_shape=[jax.ShapeDtypeStruct((b, s, GATE_LANES), F32),
                   jax.ShapeDtypeStruct((b, G_ROWS, s), F32),
                   jax.ShapeDtypeStruct((b, s, N_FOX_HEADS * GATE_LANES), BF16)],
        scratch_shapes=[pltpu.VMEM((1, GATE_LANES), F32)],
        compiler_params=_cparams(("parallel", "arbitrary")),
        name="gates",
    )(zg, bias)


def _fox_kernel(q_ref, k_ref, kx_ref, v_ref, o_ref, s_buf, p_buf, a_buf, m_ref, acc_ref, *, tq, tk):
    qi = pl.program_id(2)
    dh = FOX_HEAD_DIM
    n_diag = tq // tk
    assert n_diag % 2 == 0
    n = qi * n_diag
    lane_q = lax.broadcasted_iota(jnp.int32, (tq, dh), 1)
    lane_k = lax.broadcasted_iota(jnp.int32, (tk, dh), 1)
    q = jnp.concatenate([q_ref[0], jnp.where(lane_q < FOX_AUG, 1.0, 0.0).astype(BF16)], axis=1)
    v_ones = jnp.where(lane_k == 0, 1.0, 0.0).astype(BF16)

    def logits(slot, j):
        start = pl.multiple_of(j * tk, tk)
        k = jnp.concatenate([k_ref[0, pl.ds(start, tk), :], kx_ref[0, pl.ds(start, tk), :]], axis=1)
        s_buf[slot] = lax.dot_general(q, k, (((1,), (1,)), ((), ())), preferred_element_type=F32)

    def softmax(slot, diag):
        s = s_buf[slot]
        if diag is not None:
            r_i = lax.broadcasted_iota(jnp.int32, s.shape, 0)
            c_i = lax.broadcasted_iota(jnp.int32, s.shape, 1)
            s = jnp.where(c_i + diag * tk <= r_i, s, NEG_BIG)
        m = m_ref[...]
        m_new = jnp.maximum(m, jnp.broadcast_to(jnp.max(s, axis=-1, keepdims=True), m.shape))
        for c in range(tk // dh):
            p_buf[slot, :, c * dh:(c + 1) * dh] = jnp.exp2(s[:, c * dh:(c + 1) * dh] - m_new).astype(BF16)
        a_buf[slot] = jnp.exp2(m - m_new)
        m_ref[...] = m_new

    def accumulate(slot, j):
        start = pl.multiple_of(j * tk, tk)
        v = jnp.concatenate([v_ref[0, pl.ds(start, tk), :], v_ones], axis=1)
        pv = jnp.dot(p_buf[slot], v, preferred_element_type=F32)
        a = a_buf[slot]
        for c in range(2):
            acc_ref[:, c * dh:(c + 1) * dh] = a * acc_ref[:, c * dh:(c + 1) * dh] + pv[:, c * dh:(c + 1) * dh]

    def block_of(t):
        if isinstance(t, int):
            return n + t if t < n_diag else t - n_diag
        return jnp.where(t < n_diag, n + t, t - n_diag)

    m_ref[...] = jnp.full(m_ref.shape, NEG_BIG, F32)
    acc_ref[...] = jnp.zeros(acc_ref.shape, F32)
    logits(0, block_of(0))
    for t in range(n_diag):
        logits(1 - t % 2, block_of(t + 1))
        softmax(t % 2, t)
        if t > 0:
            accumulate(1 - t % 2, block_of(t - 1))

    def pair(i):
        t = 2 * i
        logits(1, block_of(t + 1))
        softmax(0, None)
        accumulate(1, block_of(t - 1))
        logits(0, block_of(t + 2))
        softmax(1, None)
        accumulate(0, block_of(t))

    first, n_pairs = n_diag // 2, n // 2

    def trip(g, carry):
        for u in range(FOX_PAIRS_PER_TRIP):
            pair(first + FOX_PAIRS_PER_TRIP * g + u)
        return carry

    n_trips = n_pairs // FOX_PAIRS_PER_TRIP
    lax.fori_loop(0, n_trips, trip, 0)
    done = n_trips * FOX_PAIRS_PER_TRIP
    size = FOX_PAIRS_PER_TRIP // 2
    while size >= 1:
        @pl.when((n_pairs & size) != 0)
        def _(done=done, size=size):
            for u in range(size):
                pair(first + done + u)
        done = done + (n_pairs & size)
        size //= 2

    n_steps = n + n_diag
    accumulate(1, block_of(n_steps - 1))
    acc = acc_ref[...]
    o_ref[0] = (acc[:, :dh] / acc[:, dh:dh + 1]).astype(o_ref.dtype)


def _fox(z, kx):
    b, s, _ = z.shape
    tq = min(FOX_Q_TILE, s)
    tk = min(FOX_K_TILE, tq // 2)
    h = N_FOX_HEADS
    return pl.pallas_call(
        functools.partial(_fox_kernel, tq=tq, tk=tk),
        grid=(b, h, s // tq),
        in_specs=[pl.BlockSpec((1, tq, FOX_HEAD_DIM), lambda bi, hi, qi: (bi, qi, hi)),
                  pl.BlockSpec((1, s, FOX_HEAD_DIM), lambda bi, hi, qi: (bi, 0, h + hi)),
                  pl.BlockSpec((1, s, GATE_LANES), lambda bi, hi, qi: (bi, 0, hi)),
                  pl.BlockSpec((1, s, FOX_HEAD_DIM), lambda bi, hi, qi: (bi, 0, 2 * h + hi))],
        out_specs=pl.BlockSpec((1, tq, FOX_HEAD_DIM), lambda bi, hi, qi: (bi, qi, hi)),
        out_shape=jax.ShapeDtypeStruct((b, s, FOX_WIDTH), BF16),
        scratch_shapes=[pltpu.VMEM((2, tq, tk), F32),
                        pltpu.VMEM((2, tq, tk), BF16),
                        pltpu.VMEM((2, tq, FOX_HEAD_DIM), F32),
                        pltpu.VMEM((tq, FOX_HEAD_DIM), F32),
                        pltpu.VMEM((tq, 2 * FOX_HEAD_DIM), F32)],
        compiler_params=_cparams(("parallel", "parallel", "arbitrary")),
        name="fox_attn",
    )(z, z, kx, z)


def _mlstm_kernel(qk_ref, v_ref, o_ref, gn_ref, gt_ref, cw_ref, cb_ref, ng_ref, y_ref,
                  c_st, n_st, m_st, tail, ubuf):
    c = pl.program_id(0)
    L = qk_ref.shape[1]

    @pl.when(c == 0)
    def _():
        c_st[...] = jnp.zeros_like(c_st)
        n_st[...] = jnp.zeros_like(n_st)
        m_st[...] = jnp.zeros_like(m_st)
        tail[...] = jnp.zeros_like(tail)

    r_i = lax.broadcasted_iota(jnp.int32, (L, L), 0)
    c_i = lax.broadcasted_iota(jnp.int32, (L, L), 1)
    causal = c_i <= r_i
    for bi in range(qk_ref.shape[0]):
        _mlstm_chunk(bi, causal, qk_ref, v_ref, o_ref, gn_ref, gt_ref, cw_ref, cb_ref, ng_ref, y_ref,
                     c_st, n_st, m_st, tail, ubuf)


def _mlstm_chunk(bi, causal, qk_ref, v_ref, o_ref, gn_ref, gt_ref, cw_ref, cb_ref, ng_ref, y_ref,
                 c_st, n_st, m_st, tail, ubuf):
    L = qk_ref.shape[1]
    dk, dv = MLSTM_QK_DIM, MLSTM_V_DIM
    u = qk_ref[bi].astype(F32)
    ubuf[bi, 0:8, :] = tail[bi]
    ubuf[bi, 8:8 + L, :] = u
    tail[bi] = u[L - 8:L, :]
    conv = cb_ref[...] + cw_ref[3:4, :] * u
    for d in range(1, CONV_WIDTH):
        conv = conv + cw_ref[3 - d:4 - d, :] * ubuf[bi, 8 - d:8 - d + L, :]
    qk = conv * _sigmoid(conv)

    gn = gn_ref[bi]
    gt = gt_ref[bi]

    heads = range(N_MLSTM_HEADS)
    st = [bi * N_MLSTM_HEADS + h for h in heads]
    qf = [qk[:, h * dk:(h + 1) * dk] for h in heads]
    q = [x.astype(BF16) for x in qf]
    kf = [qk[:, MLSTM_QK_WIDTH + h * dk:MLSTM_QK_WIDTH + (h + 1) * dk] * (dk ** -0.5) for h in heads]
    v = [v_ref[bi, :, h * dv:(h + 1) * dv] for h in heads]
    bcol = [gn[:, G_MF + h:G_MF + h + 1] for h in heads]
    icol = [gn[:, G_MI + h:G_MI + h + 1] for h in heads]
    brow = [gt[G_MF + h:G_MF + h + 1, :] for h in heads]
    irow = [gt[G_MI + h:G_MI + h + 1, :] for h in heads]
    m_prev = [m_st[i][:, 0:1] for i in st]
    c_prev = [c_st[i] for i in st]
    n_prev = [n_st[i] for i in st]

    m_inter = [bcol[h] + m_prev[h] for h in heads]
    dlog = [jnp.where(causal, bcol[h] - brow[h] + irow[h], NEG_BIG) for h in heads]
    m_t = [jnp.maximum(m_inter[h], jnp.max(dlog[h], axis=-1, keepdims=True)) for h in heads]
    s = [lax.dot_general(q[h], kf[h].astype(BF16), (((1,), (1,)), ((), ())), preferred_element_type=F32)
         for h in heads]
    s = [s[h] * jnp.exp(dlog[h] - m_t[h]) for h in heads]
    inter = [jnp.exp(m_inter[h] - m_t[h]) for h in heads]
    num = [inter[h] * jnp.dot(q[h], c_prev[h].astype(BF16), preferred_element_type=F32)
           + jnp.dot(s[h].astype(BF16), v[h], preferred_element_type=F32) for h in heads]
    den = [inter[h] * jnp.sum(qf[h] * n_prev[h], axis=-1, keepdims=True) + jnp.sum(s[h], axis=-1, keepdims=True)
           for h in heads]
    hh = [num[h] / jnp.maximum(jnp.abs(den[h]), jnp.exp(-m_t[h])) for h in heads]

    b_last = [bcol[h][L - 1:L, :] for h in heads]
    wlog = [b_last[h] - bcol[h] + icol[h] for h in heads]
    m_new = [jnp.maximum(b_last[h] + m_prev[h], jnp.max(wlog[h], axis=0, keepdims=True)) for h in heads]
    decay = [jnp.exp(b_last[h] + m_prev[h] - m_new[h]) for h in heads]
    wk = [kf[h] * jnp.exp(wlog[h] - m_new[h]) for h in heads]
    for h in heads:
        c_st[st[h]] = decay[h] * c_prev[h] + jnp.dot(wk[h].T.astype(BF16), v[h], preferred_element_type=F32)
        n_st[st[h]] = decay[h] * n_prev[h] + jnp.sum(wk[h], axis=0, keepdims=True)
        m_st[st[h]] = jnp.broadcast_to(m_new[h], (1, GATE_LANES))

    hn = [hh[h] * lax.rsqrt(jnp.mean(hh[h] * hh[h], axis=-1, keepdims=True) + LN_EPS) for h in heads]
    for h in heads:
        og = _sigmoid(o_ref[bi, :, h * dv:(h + 1) * dv].astype(F32))
        y_ref[bi, :, h * dv:(h + 1) * dv] = (og * (hn[h] * ng_ref[:, h * dv:(h + 1) * dv])).astype(y_ref.dtype)


def _mlstm(z, gates_n, gates_t, conv_w, conv_b, norm_g):
    b, s, _ = z.shape
    L = min(MLSTM_CHUNK, s)
    wq = 2 * MLSTM_QK_WIDTH
    assert wq == MLSTM_WIDTH == FOX_WIDTH
    base = 3 * FOX_WIDTH // wq
    return pl.pallas_call(
        _mlstm_kernel,
        grid=(s // L,),
        in_specs=[pl.BlockSpec((b, L, wq), lambda ci: (0, ci, base)),
                  pl.BlockSpec((b, L, MLSTM_WIDTH), lambda ci: (0, ci, base + 1)),
                  pl.BlockSpec((b, L, MLSTM_WIDTH), lambda ci: (0, ci, base + 2)),
                  pl.BlockSpec((b, L, GATE_LANES), lambda ci: (0, ci, 0)),
                  pl.BlockSpec((b, G_ROWS, L), lambda ci: (0, 0, ci)),
                  pl.BlockSpec((CONV_WIDTH, wq), lambda ci: (0, 0)),
                  pl.BlockSpec((1, wq), lambda ci: (0, 0)),
                  pl.BlockSpec((1, MLSTM_WIDTH), lambda ci: (0, 0))],
        out_specs=pl.BlockSpec((b, L, MLSTM_WIDTH), lambda ci: (0, ci, 0)),
        out_shape=jax.ShapeDtypeStruct((b, s, MLSTM_WIDTH), BF16),
        scratch_shapes=[pltpu.VMEM((b * N_MLSTM_HEADS, MLSTM_QK_DIM, MLSTM_V_DIM), F32),
                        pltpu.VMEM((b * N_MLSTM_HEADS, 1, MLSTM_QK_DIM), F32),
                        pltpu.VMEM((b * N_MLSTM_HEADS, 1, GATE_LANES), F32),
                        pltpu.VMEM((b, 8, wq), F32),
                        pltpu.VMEM((b, L + 8, wq), F32)],
        compiler_params=_cparams(("arbitrary",)),
        name="mlstm",
    )(z, z, z, gates_n, gates_t, conv_w, conv_b, norm_g)


def _outproj_kernel(yf_ref, ym_ref, x_ref, gt_ref, w_ref, lng_ref, lnb_ref, o_ref):
    nf = yf_ref.shape[2]
    half = x_ref.shape[1] // 2
    for r in range(2):
        rows = pl.ds(r * half, half)
        hmix = (jnp.dot(yf_ref[0, rows, :], w_ref[0:nf, :], preferred_element_type=F32)
                + jnp.dot(ym_ref[0, rows, :], w_ref[nf:, :], preferred_element_type=F32))
        y = ALPHA * x_ref[0, rows, :] + (1.0 + gt_ref[0]) * hmix
        o_ref[0, rows, :] = _ln(y) * lng_ref[...] + lnb_ref[...]


def _outproj(y_fox, y_mlstm, x, ada3, idx, w_out, ln_g, ln_b):
    b, s, d = x.shape
    tm = min(512, s)
    return pl.pallas_call(
        _outproj_kernel,
        grid=(b, s // tm),
        in_specs=[pl.BlockSpec((1, tm, FOX_WIDTH), lambda bi, i: (bi, i, 0)),
                  pl.BlockSpec((1, tm, MLSTM_WIDTH), lambda bi, i: (bi, i, 0)),
                  pl.BlockSpec((1, tm, d), lambda bi, i: (bi, i, 0)),
                  pl.BlockSpec((1, 1, d), lambda bi, i: (bi, 0, idx)),
                  pl.BlockSpec(w_out.shape, lambda bi, i: (0, 0)),
                  pl.BlockSpec((1, d), lambda bi, i: (0, 0)),
                  pl.BlockSpec((1, d), lambda bi, i: (0, 0))],
        out_specs=pl.BlockSpec((1, tm, d), lambda bi, i: (bi, i, 0)),
        out_shape=jax.ShapeDtypeStruct((b, s, d), F32),
        compiler_params=_cparams(("parallel", "parallel")),
        name="out_proj",
    )(y_fox, y_mlstm, x, ada3, w_out, ln_g, ln_b)


def _layer(x, c, w_ada, b_ada, ffn1_w_in, ffn1_w_out, ln1_g, ln1_b, w_in, fox_f_bias, mlstm_conv_w,
           mlstm_conv_b, mlstm_i_bias, mlstm_f_bias, mlstm_norm_g, w_out, ln2_g, ln2_b,
           ffn2_w_in, ffn2_w_out, ln3_g, ln3_b):
    b, s, d = x.shape
    row = lambda a: a.reshape(1, -1)

    c_pad = jnp.zeros((8, d), F32).at[:b].set(c)
    ada3 = _ada(c_pad, w_ada, row(b_ada)).reshape(8, 1, N_ADA * d)

    x = _ffn(x, ada3, 0, ffn1_w_in.astype(BF16), ffn1_w_out.astype(BF16), row(ln1_g), row(ln1_b))

    w_in = w_in.astype(BF16)
    w_mix = jnp.concatenate([w_in[:, :COL_FOX_F], w_in[:, COL_MLSTM_Q:COL_MLSTM_I], w_in[:, COL_MLSTM_O:]], axis=1)
    w_gate = jnp.concatenate([w_in[:, COL_FOX_F:COL_MLSTM_Q], w_in[:, COL_MLSTM_I:COL_MLSTM_O],
                              jnp.zeros((d, GATE_LANES - G_ROWS), BF16)], axis=1)
    gate_bias = jnp.concatenate([fox_f_bias, mlstm_i_bias, mlstm_f_bias,
                                 jnp.zeros((GATE_LANES - G_ROWS,), F32)]).reshape(1, GATE_LANES)
    z, zg = _inproj(x, ada3, 3, w_mix, w_gate)
    gates_n, gates_t, kx = _gates(zg, gate_bias)
    y_fox = _fox(z, kx)
    y_mlstm = _mlstm(z, gates_n, gates_t, mlstm_conv_w, row(mlstm_conv_b), row(mlstm_norm_g))
    x = _outproj(y_fox, y_mlstm, x, ada3, 5, w_out.astype(BF16), row(ln2_g), row(ln2_b))

    x = _ffn(x, ada3, 6, ffn2_w_in.astype(BF16), ffn2_w_out.astype(BF16), row(ln3_g), row(ln3_b))
    return x


def kernel(x, c, w_ada, b_ada, ffn1_w_in, ffn1_w_out, ln1_g, ln1_b, w_in, fox_f_bias, mlstm_conv_w,
           mlstm_conv_b, mlstm_i_bias, mlstm_f_bias, mlstm_norm_g, w_out, ln2_g, ln2_b,
           ffn2_w_in, ffn2_w_out, ln3_g, ln3_b):
    for l in range(DEPTH):
        x = _layer(x, c, w_ada[l], b_ada[l], ffn1_w_in[l], ffn1_w_out[l], ln1_g[l], ln1_b[l],
                   w_in[l], fox_f_bias[l], mlstm_conv_w[l], mlstm_conv_b[l], mlstm_i_bias[l],
                   mlstm_f_bias[l], mlstm_norm_g[l], w_out[l], ln2_g[l], ln2_b[l],
                   ffn2_w_in[l], ffn2_w_out[l], ln3_g[l], ln3_b[l])
    return x
```

```python
import functools

import jax
import jax.numpy as jnp
from jax import lax
from jax.experimental import pallas as pl
from jax.experimental.pallas import tpu as pltpu

F32 = jnp.float32
BF16 = jnp.bfloat16

D_MODEL = 2048
DEPTH = 1
N_FOX_HEADS = 8
FOX_HEAD_DIM = 128
FOX_WIDTH = N_FOX_HEADS * FOX_HEAD_DIM
N_MLSTM_HEADS = 4
MLSTM_V_DIM = 256
MLSTM_QK_DIM = 128
MLSTM_WIDTH = N_MLSTM_HEADS * MLSTM_V_DIM
MLSTM_QK_WIDTH = N_MLSTM_HEADS * MLSTM_QK_DIM
CONV_WIDTH = 4
D_FF = 5632
N_ADA = 9
ALPHA = (2 * DEPTH) ** 0.25
LN_EPS = 1e-5

COL_FOX_F = 3 * FOX_WIDTH
COL_MLSTM_Q = COL_FOX_F + N_FOX_HEADS
COL_MLSTM_I = COL_MLSTM_Q + 2 * MLSTM_QK_WIDTH + MLSTM_WIDTH
COL_MLSTM_O = COL_MLSTM_I + 2 * N_MLSTM_HEADS
IN_WIDTH = COL_MLSTM_O + MLSTM_WIDTH

GATE_LANES = 128
G_FOX = 0
G_MI = N_FOX_HEADS
G_MF = G_MI + N_MLSTM_HEADS
G_ROWS = 16
MIX_COLS = 3 * FOX_WIDTH + 2 * MLSTM_QK_WIDTH + 2 * MLSTM_WIDTH

MLSTM_CHUNK = 256
GATE_TILE = 1024
NEG_BIG = -1e30
LOG2E = 1.4426950408889634
FOX_AUG = 3
FOX_Q_TILE = 1024
FOX_K_TILE = 512
FOX_PAIRS_PER_TRIP = 4
VMEM_LIMIT = 60 * 1024 * 1024


def _cparams(sem):
    return pltpu.CompilerParams(dimension_semantics=sem, vmem_limit_bytes=VMEM_LIMIT)


def _ln(x):
    mu = jnp.mean(x, axis=-1, keepdims=True)
    xc = x - mu
    var = jnp.mean(xc * xc, axis=-1, keepdims=True)
    return xc * lax.rsqrt(var + LN_EPS)


def _sigmoid(x):
    return 0.5 * jnp.tanh(0.5 * x) + 0.5


def _log_sigmoid(x):
    return jnp.minimum(x, 0.0) - jnp.log1p(jnp.exp(-jnp.abs(x)))


def _ada_kernel(c_ref, w_ref, b_ref, o_ref):
    c = c_ref[...]
    s = (c * _sigmoid(c)).astype(BF16)
    o_ref[...] = jnp.dot(s, w_ref[...].astype(BF16), preferred_element_type=F32) + b_ref[...]


def _ada(c_pad, w_ada, b_ada):
    rows, d = c_pad.shape
    n = w_ada.shape[1]
    tn = 1024
    return pl.pallas_call(
        _ada_kernel,
        grid=(n // tn,),
        in_specs=[pl.BlockSpec((rows, d), lambda j: (0, 0)),
                  pl.BlockSpec((d, tn), lambda j: (0, j)),
                  pl.BlockSpec((1, tn), lambda j: (0, j))],
        out_specs=pl.BlockSpec((rows, tn), lambda j: (0, j)),
        out_shape=jax.ShapeDtypeStruct((rows, n), F32),
        compiler_params=_cparams(("arbitrary",)),
        name="ada_proj",
    )(c_pad, w_ada, b_ada)


def _ffn_kernel(x_ref, sh_ref, sc_ref, gt_ref, wg_ref, wu_ref, wo_ref, lng_ref, lnb_ref, o_ref, h_ref):
    k = pl.program_id(2)
    last = pl.num_programs(2) - 1

    def swiglu_part(h):
        g = jnp.dot(h, wg_ref[...], preferred_element_type=F32)
        u = jnp.dot(h, wu_ref[...], preferred_element_type=F32)
        a = (g * _sigmoid(g) * u).astype(BF16)
        return jnp.dot(a, wo_ref[...], preferred_element_type=F32)

    halves = [pl.ds(r * (x_ref.shape[1] // 2), x_ref.shape[1] // 2) for r in range(2)]

    @pl.when(k == 0)
    def _():
        for rows in halves:
            h = (_ln(x_ref[0, rows, :]) * (1.0 + sc_ref[0]) + sh_ref[0]).astype(BF16)
            h_ref[rows, :] = h
            o_ref[0, rows, :] = swiglu_part(h)

    @pl.when((k > 0) & (k < last))
    def _():
        o_ref[0] += swiglu_part(h_ref[...])

    @pl.when(k == last)
    def _():
        for rows in halves:
            y = (ALPHA * x_ref[0, rows, :]
                 + (0.5 * (1.0 + gt_ref[0])) * (o_ref[0, rows, :] + swiglu_part(h_ref[rows, :])))
            o_ref[0, rows, :] = _ln(y) * lng_ref[...] + lnb_ref[...]


def _ffn(x, ada3, idx, w_in, w_out, ln_g, ln_b):
    b, s, d = x.shape
    dff = w_out.shape[0]
    tm = min(1024, s)
    tf = 512
    nk = dff // tf
    vec = lambda j: pl.BlockSpec((1, 1, d), lambda bi, i, k: (bi, 0, j))
    return pl.pallas_call(
        _ffn_kernel,
        grid=(b, s // tm, nk),
        in_specs=[pl.BlockSpec((1, tm, d), lambda bi, i, k: (bi, i, 0)),
                  vec(idx), vec(idx + 1), vec(idx + 2),
                  pl.BlockSpec((d, tf), lambda bi, i, k: (0, k)),
                  pl.BlockSpec((d, tf), lambda bi, i, k: (0, k + nk)),
                  pl.BlockSpec((tf, d), lambda bi, i, k: (k, 0)),
                  pl.BlockSpec((1, d), lambda bi, i, k: (0, 0)),
                  pl.BlockSpec((1, d), lambda bi, i, k: (0, 0))],
        out_specs=pl.BlockSpec((1, tm, d), lambda bi, i, k: (bi, i, 0)),
        out_shape=jax.ShapeDtypeStruct((b, s, d), F32),
        scratch_shapes=[pltpu.VMEM((tm, d), BF16)],
        compiler_params=_cparams(("parallel", "parallel", "arbitrary")),
        name="ffn",
    )(x, ada3, ada3, ada3, w_in, w_in, w_out, ln_g, ln_b)


def _inproj_kernel(x_ref, sh_ref, sc_ref, w_ref, cs_ref, wgate_ref, z_ref, zg_ref, h_ref):
    j = pl.program_id(2)

    def project(h):
        z = jnp.dot(h, w_ref[...], preferred_element_type=F32)
        z_ref[0] = (z * cs_ref[...]).astype(BF16)

    @pl.when(j == 0)
    def _():
        h = (_ln(x_ref[0]) * (1.0 + sc_ref[0]) + sh_ref[0]).astype(BF16)
        h_ref[...] = h
        zg_ref[0] = jnp.dot(h, wgate_ref[...], preferred_element_type=F32)
        project(h)

    @pl.when(j > 0)
    def _():
        project(h_ref[...])


def _inproj(x, ada3, idx, w_mix, w_gate):
    b, s, d = x.shape
    n = w_mix.shape[1]
    tm = min(1024, s)
    tn = 2048
    vec = lambda j: pl.BlockSpec((1, 1, d), lambda bi, i, jj: (bi, 0, j))
    col_scale = jnp.where(jnp.arange(n) < FOX_WIDTH, FOX_HEAD_DIM ** -0.5 * LOG2E, 1.0).astype(F32).reshape(1, n)
    return pl.pallas_call(
        _inproj_kernel,
        grid=(b, s // tm, n // tn),
        in_specs=[pl.BlockSpec((1, tm, d), lambda bi, i, j: (bi, i, 0)),
                  vec(idx), vec(idx + 1),
                  pl.BlockSpec((d, tn), lambda bi, i, j: (0, j)),
                  pl.BlockSpec((1, tn), lambda bi, i, j: (0, j)),
                  pl.BlockSpec((d, GATE_LANES), lambda bi, i, j: (0, 0))],
        out_specs=[pl.BlockSpec((1, tm, tn), lambda bi, i, j: (bi, i, j)),
                   pl.BlockSpec((1, tm, GATE_LANES), lambda bi, i, j: (bi, i, 0))],
        out_shape=[jax.ShapeDtypeStruct((b, s, n), BF16),
                   jax.ShapeDtypeStruct((b, s, GATE_LANES), F32)],
        scratch_shapes=[pltpu.VMEM((tm, d), BF16)],
        compiler_params=_cparams(("parallel", "parallel", "arbitrary")),
        name="in_proj",
    )(x, ada3, ada3, w_mix, col_scale, w_gate)


def _split3(v):
    hi = v.astype(BF16)
    r = v - hi.astype(F32)
    mid = r.astype(BF16)
    lo = (r - mid.astype(F32)).astype(BF16)
    return hi, mid, lo


def _gate_kernel(zg_ref, bias_ref, nat_ref, t_ref, kx_ref, carry_ref, *, chunk):
    i = pl.program_id(1)

    @pl.when(i == 0)
    def _():
        carry_ref[...] = jnp.zeros_like(carry_ref)

    rows = chunk
    col = lax.broadcasted_iota(jnp.int32, (rows, GATE_LANES), 1)
    is_in_gate = (col >= G_MI) & (col < G_MF)
    r_i = lax.broadcasted_iota(jnp.int32, (rows, rows), 0)
    c_i = lax.broadcasted_iota(jnp.int32, (rows, rows), 1)
    tri = jnp.where(r_i >= c_i, 1.0, 0.0).astype(BF16)
    for c in range(zg_ref.shape[1] // rows):
        span = pl.ds(c * rows, rows)
        z = zg_ref[0, span, :] + bias_ref[...]
        v = jnp.where(is_in_gate, z, _log_sigmoid(z))
        hi, mid, lo = _split3(v)
        csum = (jnp.dot(tri, hi, preferred_element_type=F32)
                + jnp.dot(tri, mid, preferred_element_type=F32)
                + jnp.dot(tri, lo, preferred_element_type=F32))
        running = csum + carry_ref[...]
        carry_ref[...] = running[rows - 1:rows, :]
        out = jnp.where(col < G_MI, running, jnp.where(is_in_gate, v, csum))
        nat_ref[0, span, :] = out
        t_ref[0, :, span] = out.T[:G_ROWS, :]
        for h in range(N_FOX_HEADS):
            parts = _split3(jnp.broadcast_to(running[:, h:h + 1] * (-LOG2E), z.shape))
            tile = jnp.zeros(z.shape, F32)
            for t in reversed(range(FOX_AUG)):
                tile = jnp.where(col == t, parts[t].astype(F32), tile)
            kx_ref[0, span, h * GATE_LANES:(h + 1) * GATE_LANES] = tile.astype(BF16)


def _gates(zg, bias):
    b, s, _ = zg.shape
    chunk = min(MLSTM_CHUNK, s)
    tl = min(GATE_TILE, s)
    return pl.pallas_call(
        functools.partial(_gate_kernel, chunk=chunk),
        grid=(b, s // tl),
        in_specs=[pl.BlockSpec((1, tl, GATE_LANES), lambda bi, i: (bi, i, 0)),
                  pl.BlockSpec((1, GATE_LANES), lambda bi, i: (0, 0))],
        out_specs=[pl.BlockSpec((1, tl, GATE_LANES), lambda bi, i: (bi, i, 0)),
                   pl.BlockSpec((1, G_ROWS, tl), lambda bi, i: (bi, 0, i)),
                   pl.BlockSpec((1, tl, N_FOX_HEADS * GATE_LANES), lambda bi, i: (bi, i, 0))],
        out_shape=[jax.ShapeDtypeStruct((b, s, GATE_LANES), F32),
                   jax.ShapeDtypeStruct((b, G_ROWS, s), F32),
                   jax.ShapeDtypeStruct((b, s, N_FOX_HEADS * GATE_LANES), BF16)],
        scratch_shapes=[pltpu.VMEM((1, GATE_LANES), F32)],
        compiler_params=_cparams(("parallel", "arbitrary")),
        name="gates",
    )(zg, bias)


def _fox_kernel(q_ref, k_ref, kx_ref, v_ref, o_ref, s_buf, p_buf, a_buf, m_ref, acc_ref, *, tq, tk):
    qi = pl.program_id(2)
    dh = FOX_HEAD_DIM
    n_diag = tq // tk
    assert n_diag % 2 == 0
    n = qi * n_diag
    lane_q = lax.broadcasted_iota(jnp.int32, (tq, dh), 1)
    lane_k = lax.broadcasted_iota(jnp.int32, (tk, dh), 1)
    q = jnp.concatenate([q_ref[0], jnp.where(lane_q < FOX_AUG, 1.0, 0.0).astype(BF16)], axis=1)
    v_ones = jnp.where(lane_k == 0, 1.0, 0.0).astype(BF16)

    def logits(slot, j):
        start = pl.multiple_of(j * tk, tk)
        k = jnp.concatenate([k_ref[0, pl.ds(start, tk), :], kx_ref[0, pl.ds(start, tk), :]], axis=1)
        s_buf[slot] = lax.dot_general(q, k, (((1,), (1,)), ((), ())), preferred_element_type=F32)

    def softmax(slot, diag):
        s = s_buf[slot]
        if diag is not None:
            r_i = lax.broadcasted_iota(jnp.int32, s.shape, 0)
            c_i = lax.broadcasted_iota(jnp.int32, s.shape, 1)
            s = jnp.where(c_i + diag * tk <= r_i, s, NEG_BIG)
        m = m_ref[...]
        m_new = jnp.maximum(m, jnp.broadcast_to(jnp.max(s, axis=-1, keepdims=True), m.shape))
        for c in range(tk // dh):
            p_buf[slot, :, c * dh:(c + 1) * dh] = jnp.exp2(s[:, c * dh:(c + 1) * dh] - m_new).astype(BF16)
        a_buf[slot] = jnp.exp2(m - m_new)
        m_ref[...] = m_new

    def accumulate(slot, j):
        start = pl.multiple_of(j * tk, tk)
        v = jnp.concatenate([v_ref[0, pl.ds(start, tk), :], v_ones], axis=1)
        pv = jnp.dot(p_buf[slot], v, preferred_element_type=F32)
        a = a_buf[slot]
        for c in range(2):
            acc_ref[:, c * dh:(c + 1) * dh] = a * acc_ref[:, c * dh:(c + 1) * dh] + pv[:, c * dh:(c + 1) * dh]

    def block_of(t):
        if isinstance(t, int):
            return n + t if t < n_diag else t - n_diag
        return jnp.where(t < n_diag, n + t, t - n_diag)

    m_ref[...] = jnp.full(m_ref.shape, NEG_BIG, F32)
    acc_ref[...] = jnp.zeros(acc_ref.shape, F32)
    logits(0, block_of(0))
    for t in range(n_diag):
        logits(1 - t % 2, block_of(t + 1))
        softmax(t % 2, t)
        if t > 0:
            accumulate(1 - t % 2, block_of(t - 1))

    def pair(i):
        t = 2 * i
        softmax(0, None)
        logits(1, block_of(t + 1))
        accumulate(1, block_of(t - 1))
        softmax(1, None)
        logits(0, block_of(t + 2))
        accumulate(0, block_of(t))

    first, n_pairs = n_diag // 2, n // 2

    def trip(g, carry):
        for u in range(FOX_PAIRS_PER_TRIP):
            pair(first + FOX_PAIRS_PER_TRIP * g + u)
        return carry

    n_trips = n_pairs // FOX_PAIRS_PER_TRIP
    lax.fori_loop(0, n_trips, trip, 0)
    done = n_trips * FOX_PAIRS_PER_TRIP
    size = FOX_PAIRS_PER_TRIP // 2
    while size >= 1:
        @pl.when((n_pairs & size) != 0)
        def _(done=done, size=size):
            for u in range(size):
                pair(first + done + u)
        done = done + (n_pairs & size)
        size //= 2

    n_steps = n + n_diag
    accumulate(1, block_of(n_steps - 1))
    acc = acc_ref[...]
    o_ref[0] = (acc[:, :dh] / acc[:, dh:dh + 1]).astype(o_ref.dtype)


def _fox(z, kx):
    b, s, _ = z.shape
    tq = min(FOX_Q_TILE, s)
    tk = min(FOX_K_TILE, tq // 2)
    h = N_FOX_HEADS
    return pl.pallas_call(
        functools.partial(_fox_kernel, tq=tq, tk=tk),
        grid=(b, h, s // tq),
        in_specs=[pl.BlockSpec((1, tq, FOX_HEAD_DIM), lambda bi, hi, qi: (bi, qi, hi)),
                  pl.BlockSpec((1, s, FOX_HEAD_DIM), lambda bi, hi, qi: (bi, 0, h + hi)),
                  pl.BlockSpec((1, s, GATE_LANES), lambda bi, hi, qi: (bi, 0, hi)),
                  pl.BlockSpec((1, s, FOX_HEAD_DIM), lambda bi, hi, qi: (bi, 0, 2 * h + hi))],
        out_specs=pl.BlockSpec((1, tq, FOX_HEAD_DIM), lambda bi, hi, qi: (bi, qi, hi)),
        out_shape=jax.ShapeDtypeStruct((b, s, FOX_WIDTH), BF16),
        scratch_shapes=[pltpu.VMEM((2, tq, tk), F32),
                        pltpu.VMEM((2, tq, tk), BF16),
                        pltpu.VMEM((2, tq, FOX_HEAD_DIM), F32),
                        pltpu.VMEM((tq, FOX_HEAD_DIM), F32),
                        pltpu.VMEM((tq, 2 * FOX_HEAD_DIM), F32)],
        compiler_params=_cparams(("parallel", "parallel", "arbitrary")),
        name="fox_attn",
    )(z, z, kx, z)


def _mlstm_kernel(qk_ref, v_ref, o_ref, gn_ref, gt_ref, cw_ref, cb_ref, ng_ref, y_ref,
                  c_st, n_st, m_st, tail, ubuf):
    c = pl.program_id(0)
    L = qk_ref.shape[1]

    @pl.when(c == 0)
    def _():
        c_st[...] = jnp.zeros_like(c_st)
        n_st[...] = jnp.zeros_like(n_st)
        m_st[...] = jnp.zeros_like(m_st)
        tail[...] = jnp.zeros_like(tail)

    r_i = lax.broadcasted_iota(jnp.int32, (L, L), 0)
    c_i = lax.broadcasted_iota(jnp.int32, (L, L), 1)
    causal = c_i <= r_i
    _mlstm_chunk(causal, qk_ref, v_ref, o_ref, gn_ref, gt_ref, cw_ref, cb_ref, ng_ref, y_ref,
                 c_st, n_st, m_st, tail, ubuf)


def _mlstm_chunk(causal, qk_ref, v_ref, o_ref, gn_ref, gt_ref, cw_ref, cb_ref, ng_ref, y_ref,
                 c_st, n_st, m_st, tail, ubuf):
    nb, L = qk_ref.shape[0], qk_ref.shape[1]
    dk, dv = MLSTM_QK_DIM, MLSTM_V_DIM
    qk = []
    for bi in range(nb):
        u = qk_ref[bi].astype(F32)
        ubuf[bi, 0:8, :] = tail[bi]
        ubuf[bi, 8:8 + L, :] = u
        tail[bi] = u[L - 8:L, :]
        conv = cb_ref[...] + cw_ref[3:4, :] * u
        for d in range(1, CONV_WIDTH):
            conv = conv + cw_ref[3 - d:4 - d, :] * ubuf[bi, 8 - d:8 - d + L, :]
        qk.append(conv * _sigmoid(conv))
    gn = [gn_ref[bi] for bi in range(nb)]
    gt = [gt_ref[bi] for bi in range(nb)]

    heads = range(nb * N_MLSTM_HEADS)
    bh = [divmod(i, N_MLSTM_HEADS) for i in heads]
    qf = [qk[bi][:, h * dk:(h + 1) * dk] for bi, h in bh]
    q = [x.astype(BF16) for x in qf]
    kf = [qk[bi][:, MLSTM_QK_WIDTH + h * dk:MLSTM_QK_WIDTH + (h + 1) * dk] * (dk ** -0.5) for bi, h in bh]
    v = [v_ref[bi, :, h * dv:(h + 1) * dv] for bi, h in bh]
    bcol = [gn[bi][:, G_MF + h:G_MF + h + 1] for bi, h in bh]
    icol = [gn[bi][:, G_MI + h:G_MI + h + 1] for bi, h in bh]
    brow = [gt[bi][G_MF + h:G_MF + h + 1, :] for bi, h in bh]
    irow = [gt[bi][G_MI + h:G_MI + h + 1, :] for bi, h in bh]
    m_prev = [m_st[i][:, 0:1] for i in heads]
    c_prev = [c_st[i] for i in heads]
    n_prev = [n_st[i] for i in heads]

    m_inter = [bcol[h] + m_prev[h] for h in heads]
    dlog = [jnp.where(causal, bcol[h] - brow[h] + irow[h], NEG_BIG) for h in heads]
    m_t = [jnp.maximum(m_inter[h], jnp.max(dlog[h], axis=-1, keepdims=True)) for h in heads]
    s = [lax.dot_general(q[h], kf[h].astype(BF16), (((1,), (1,)), ((), ())), preferred_element_type=F32)
         for h in heads]
    s = [s[h] * jnp.exp(dlog[h] - m_t[h]) for h in heads]
    inter = [jnp.exp(m_inter[h] - m_t[h]) for h in heads]
    num = [inter[h] * jnp.dot(q[h], c_prev[h].astype(BF16), preferred_element_type=F32)
           + jnp.dot(s[h].astype(BF16), v[h], preferred_element_type=F32) for h in heads]
    den = [inter[h] * jnp.sum(qf[h] * n_prev[h], axis=-1, keepdims=True) + jnp.sum(s[h], axis=-1, keepdims=True)
           for h in heads]
    hh = [num[h] / jnp.maximum(jnp.abs(den[h]), jnp.exp(-m_t[h])) for h in heads]

    b_last = [bcol[h][L - 1:L, :] for h in heads]
    wlog = [b_last[h] - bcol[h] + icol[h] for h in heads]
    m_new = [jnp.maximum(b_last[h] + m_prev[h], jnp.max(wlog[h], axis=0, keepdims=True)) for h in heads]
    decay = [jnp.exp(b_last[h] + m_prev[h] - m_new[h]) for h in heads]
    wk = [kf[h] * jnp.exp(wlog[h] - m_new[h]) for h in heads]
    for i in heads:
        c_st[i] = decay[i] * c_prev[i] + jnp.dot(wk[i].T.astype(BF16), v[i], preferred_element_type=F32)
        n_st[i] = decay[i] * n_prev[i] + jnp.sum(wk[i], axis=0, keepdims=True)
        m_st[i] = jnp.broadcast_to(m_new[i], (1, GATE_LANES))

    hn = [hh[i] * lax.rsqrt(jnp.mean(hh[i] * hh[i], axis=-1, keepdims=True) + LN_EPS) for i in heads]
    for i, (bi, h) in zip(heads, bh):
        og = _sigmoid(o_ref[bi, :, h * dv:(h + 1) * dv].astype(F32))
        y_ref[bi, :, h * dv:(h + 1) * dv] = (og * (hn[i] * ng_ref[:, h * dv:(h + 1) * dv])).astype(y_ref.dtype)


def _mlstm(z, gates_n, gates_t, conv_w, conv_b, norm_g):
    b, s, _ = z.shape
    L = min(MLSTM_CHUNK, s)
    wq = 2 * MLSTM_QK_WIDTH
    assert wq == MLSTM_WIDTH == FOX_WIDTH
    base = 3 * FOX_WIDTH // wq
    return pl.pallas_call(
        _mlstm_kernel,
        grid=(s // L,),
        in_specs=[pl.BlockSpec((b, L, wq), lambda ci: (0, ci, base)),
                  pl.BlockSpec((b, L, MLSTM_WIDTH), lambda ci: (0, ci, base + 1)),
                  pl.BlockSpec((b, L, MLSTM_WIDTH), lambda ci: (0, ci, base + 2)),
                  pl.BlockSpec((b, L, GATE_LANES), lambda ci: (0, ci, 0)),
                  pl.BlockSpec((b, G_ROWS, L), lambda ci: (0, 0, ci)),
                  pl.BlockSpec((CONV_WIDTH, wq), lambda ci: (0, 0)),
                  pl.BlockSpec((1, wq), lambda ci: (0, 0)),
                  pl.BlockSpec((1, MLSTM_WIDTH), lambda ci: (0, 0))],
        out_specs=pl.BlockSpec((b, L, MLSTM_WIDTH), lambda ci: (0, ci, 0)),
        out_shape=jax.ShapeDtypeStruct((b, s, MLSTM_WIDTH), BF16),
        scratch_shapes=[pltpu.VMEM((b * N_MLSTM_HEADS, MLSTM_QK_DIM, MLSTM_V_DIM), F32),
                        pltpu.VMEM((b * N_MLSTM_HEADS, 1, MLSTM_QK_DIM), F32),
                        pltpu.VMEM((b * N_MLSTM_HEADS, 1, GATE_LANES), F32),
                        pltpu.VMEM((b, 8, wq), F32),
                        pltpu.VMEM((b, L + 8, wq), F32)],
        compiler_params=_cparams(("arbitrary",)),
        name="mlstm",
    )(z, z, z, gates_n, gates_t, conv_w, conv_b, norm_g)


def _outproj_kernel(yf_ref, ym_ref, x_ref, gt_ref, w_ref, lng_ref, lnb_ref, o_ref):
    nf = yf_ref.shape[2]
    half = x_ref.shape[1] // 2
    halves = [pl.ds(r * half, half) for r in range(2)]
    hmix = [jnp.dot(yf_ref[0, rows, :], w_ref[0:nf, :], preferred_element_type=F32)
            + jnp.dot(ym_ref[0, rows, :], w_ref[nf:, :], preferred_element_type=F32) for rows in halves]
    for rows, hm in zip(halves, hmix):
        y = ALPHA * x_ref[0, rows, :] + (1.0 + gt_ref[0]) * hm
        o_ref[0, rows, :] = _ln(y) * lng_ref[...] + lnb_ref[...]


def _outproj(y_fox, y_mlstm, x, ada3, idx, w_out, ln_g, ln_b):
    b, s, d = x.shape
    tm = min(512, s)
    return pl.pallas_call(
        _outproj_kernel,
        grid=(b, s // tm),
        in_specs=[pl.BlockSpec((1, tm, FOX_WIDTH), lambda bi, i: (bi, i, 0)),
                  pl.BlockSpec((1, tm, MLSTM_WIDTH), lambda bi, i: (bi, i, 0)),
                  pl.BlockSpec((1, tm, d), lambda bi, i: (bi, i, 0)),
                  pl.BlockSpec((1, 1, d), lambda bi, i: (bi, 0, idx)),
                  pl.BlockSpec(w_out.shape, lambda bi, i: (0, 0)),
                  pl.BlockSpec((1, d), lambda bi, i: (0, 0)),
                  pl.BlockSpec((1, d), lambda bi, i: (0, 0))],
        out_specs=pl.BlockSpec((1, tm, d), lambda bi, i: (bi, i, 0)),
        out_shape=jax.ShapeDtypeStruct((b, s, d), F32),
        compiler_params=_cparams(("parallel", "parallel")),
        name="out_proj",
    )(y_fox, y_mlstm, x, ada3, w_out, ln_g, ln_b)


def _layer(x, c, w_ada, b_ada, ffn1_w_in, ffn1_w_out, ln1_g, ln1_b, w_in, fox_f_bias, mlstm_conv_w,
           mlstm_conv_b, mlstm_i_bias, mlstm_f_bias, mlstm_norm_g, w_out, ln2_g, ln2_b,
           ffn2_w_in, ffn2_w_out, ln3_g, ln3_b):
    b, s, d = x.shape
    row = lambda a: a.reshape(1, -1)

    c_pad = jnp.zeros((8, d), F32).at[:b].set(c)
    ada3 = _ada(c_pad, w_ada, row(b_ada)).reshape(8, 1, N_ADA * d)

    x = _ffn(x, ada3, 0, ffn1_w_in.astype(BF16), ffn1_w_out.astype(BF16), row(ln1_g), row(ln1_b))

    w_in = w_in.astype(BF16)
    w_mix = jnp.concatenate([w_in[:, :COL_FOX_F], w_in[:, COL_MLSTM_Q:COL_MLSTM_I], w_in[:, COL_MLSTM_O:]], axis=1)
    w_gate = jnp.concatenate([w_in[:, COL_FOX_F:COL_MLSTM_Q], w_in[:, COL_MLSTM_I:COL_MLSTM_O],
                              jnp.zeros((d, GATE_LANES - G_ROWS), BF16)], axis=1)
    gate_bias = jnp.concatenate([fox_f_bias, mlstm_i_bias, mlstm_f_bias,
                                 jnp.zeros((GATE_LANES - G_ROWS,), F32)]).reshape(1, GATE_LANES)
    z, zg = _inproj(x, ada3, 3, w_mix, w_gate)
    gates_n, gates_t, kx = _gates(zg, gate_bias)
    y_fox = _fox(z, kx)
    y_mlstm = _mlstm(z, gates_n, gates_t, mlstm_conv_w, row(mlstm_conv_b), row(mlstm_norm_g))
    x = _outproj(y_fox, y_mlstm, x, ada3, 5, w_out.astype(BF16), row(ln2_g), row(ln2_b))

    x = _ffn(x, ada3, 6, ffn2_w_in.astype(BF16), ffn2_w_out.astype(BF16), row(ln3_g), row(ln3_b))
    return x


def kernel(x, c, w_ada, b_ada, ffn1_w_in, ffn1_w_out, ln1_g, ln1_b, w_in, fox_f_bias, mlstm_conv_w,
           mlstm_conv_b, mlstm_i_bias, mlstm_f_bias, mlstm_norm_g, w_out, ln2_g, ln2_b,
           ffn2_w_in, ffn2_w_out, ln3_g, ln3_b):
    for l in range(DEPTH):
        x = _layer(x, c, w_ada[l], b_ada[l], ffn1_w_in[l], ffn1_w_out[l], ln1_g[l], ln1_b[l],
                   w_in[l], fox_f_bias[l], mlstm_conv_w[l], mlstm_conv_b[l], mlstm_i_bias[l],
                   mlstm_f_bias[l], mlstm_norm_g[l], w_out[l], ln2_g[l], ln2_b[l],
                   ffn2_w_in[l], ffn2_w_out[l], ln3_g[l], ln3_b[l])
    return x
```

```python
import functools

import jax
import jax.numpy as jnp
from jax import lax
from jax.experimental import pallas as pl
from jax.experimental.pallas import tpu as pltpu

F32 = jnp.float32
BF16 = jnp.bfloat16

D_MODEL = 2048
DEPTH = 1
N_FOX_HEADS = 8
FOX_HEAD_DIM = 128
FOX_WIDTH = N_FOX_HEADS * FOX_HEAD_DIM
N_MLSTM_HEADS = 4
MLSTM_V_DIM = 256
MLSTM_QK_DIM = 128
MLSTM_WIDTH = N_MLSTM_HEADS * MLSTM_V_DIM
MLSTM_QK_WIDTH = N_MLSTM_HEADS * MLSTM_QK_DIM
CONV_WIDTH = 4
D_FF = 5632
N_ADA = 9
ALPHA = (2 * DEPTH) ** 0.25
LN_EPS = 1e-5

COL_FOX_F = 3 * FOX_WIDTH
COL_MLSTM_Q = COL_FOX_F + N_FOX_HEADS
COL_MLSTM_I = COL_MLSTM_Q + 2 * MLSTM_QK_WIDTH + MLSTM_WIDTH
COL_MLSTM_O = COL_MLSTM_I + 2 * N_MLSTM_HEADS
IN_WIDTH = COL_MLSTM_O + MLSTM_WIDTH

GATE_LANES = 128
G_FOX = 0
G_MI = N_FOX_HEADS
G_MF = G_MI + N_MLSTM_HEADS
G_ROWS = 16
MIX_COLS = 3 * FOX_WIDTH + 2 * MLSTM_QK_WIDTH + 2 * MLSTM_WIDTH

ADA_COL_TILE = 1024
FFN_ROW_TILE = 1024
FFN_FF_TILE = 512
INPROJ_ROW_TILE = 1024
INPROJ_COL_TILE = 2048
OUTPROJ_ROW_TILE = 512
MLSTM_CHUNK = 256
GATE_TILE = 1024
NEG_BIG = -1e30
LOG2E = 1.4426950408889634
FOX_AUG = 3
FOX_Q_TILE = 1024
FOX_K_TILE = 256
FOX_PAIRS_PER_TRIP = 8
VMEM_LIMIT = 60 * 1024 * 1024


def _cparams(sem):
    return pltpu.CompilerParams(dimension_semantics=sem, vmem_limit_bytes=VMEM_LIMIT)


def _ln(x):
    mu = jnp.mean(x, axis=-1, keepdims=True)
    xc = x - mu
    var = jnp.mean(xc * xc, axis=-1, keepdims=True)
    return xc * lax.rsqrt(var + LN_EPS)


def _sigmoid(x):
    return 0.5 * jnp.tanh(0.5 * x) + 0.5


def _log_sigmoid(x):
    return jnp.minimum(x, 0.0) - jnp.log1p(jnp.exp(-jnp.abs(x)))


def _ada_kernel(c_ref, w_ref, b_ref, o_ref):
    c = c_ref[...]
    s = (c * _sigmoid(c)).astype(BF16)
    o_ref[...] = jnp.dot(s, w_ref[...].astype(BF16), preferred_element_type=F32) + b_ref[...]


def _ada(c_pad, w_ada, b_ada):
    rows, d = c_pad.shape
    n = w_ada.shape[1]
    tn = ADA_COL_TILE
    return pl.pallas_call(
        _ada_kernel,
        grid=(n // tn,),
        in_specs=[pl.BlockSpec((rows, d), lambda j: (0, 0)),
                  pl.BlockSpec((d, tn), lambda j: (0, j)),
                  pl.BlockSpec((1, tn), lambda j: (0, j))],
        out_specs=pl.BlockSpec((rows, tn), lambda j: (0, j)),
        out_shape=jax.ShapeDtypeStruct((rows, n), F32),
        compiler_params=_cparams(("arbitrary",)),
        name="ada_proj",
    )(c_pad, w_ada, b_ada)


def _ffn_kernel(x_ref, sh_ref, sc_ref, gt_ref, wg_ref, wu_ref, wo_ref, lng_ref, lnb_ref, o_ref, h_ref):
    k = pl.program_id(2)
    last = pl.num_programs(2) - 1

    def swiglu_part(h):
        g = jnp.dot(h, wg_ref[...], preferred_element_type=F32)
        u = jnp.dot(h, wu_ref[...], preferred_element_type=F32)
        a = (g * _sigmoid(g) * u).astype(BF16)
        return jnp.dot(a, wo_ref[...], preferred_element_type=F32)

    halves = [pl.ds(r * (x_ref.shape[1] // 2), x_ref.shape[1] // 2) for r in range(2)]

    @pl.when(k == 0)
    def _():
        for rows in halves:
            h = (_ln(x_ref[0, rows, :]) * (1.0 + sc_ref[0]) + sh_ref[0]).astype(BF16)
            h_ref[rows, :] = h
            o_ref[0, rows, :] = swiglu_part(h)

    @pl.when((k > 0) & (k < last))
    def _():
        o_ref[0] += swiglu_part(h_ref[...])

    @pl.when(k == last)
    def _():
        for rows in halves:
            y = (ALPHA * x_ref[0, rows, :]
                 + (0.5 * (1.0 + gt_ref[0])) * (o_ref[0, rows, :] + swiglu_part(h_ref[rows, :])))
            o_ref[0, rows, :] = _ln(y) * lng_ref[...] + lnb_ref[...]


def _ffn(x, ada3, idx, w_in, w_out, ln_g, ln_b):
    b, s, d = x.shape
    dff = w_out.shape[0]
    tm = min(FFN_ROW_TILE, s)
    tf = FFN_FF_TILE
    nk = dff // tf
    vec = lambda j: pl.BlockSpec((1, 1, d), lambda bi, i, k: (bi, 0, j))
    return pl.pallas_call(
        _ffn_kernel,
        grid=(b, s // tm, nk),
        in_specs=[pl.BlockSpec((1, tm, d), lambda bi, i, k: (bi, i, 0)),
                  vec(idx), vec(idx + 1), vec(idx + 2),
                  pl.BlockSpec((d, tf), lambda bi, i, k: (0, k)),
                  pl.BlockSpec((d, tf), lambda bi, i, k: (0, k + nk)),
                  pl.BlockSpec((tf, d), lambda bi, i, k: (k, 0)),
                  pl.BlockSpec((1, d), lambda bi, i, k: (0, 0)),
                  pl.BlockSpec((1, d), lambda bi, i, k: (0, 0))],
        out_specs=pl.BlockSpec((1, tm, d), lambda bi, i, k: (bi, i, 0)),
        out_shape=jax.ShapeDtypeStruct((b, s, d), F32),
        scratch_shapes=[pltpu.VMEM((tm, d), BF16)],
        compiler_params=_cparams(("parallel", "parallel", "arbitrary")),
        name="ffn",
    )(x, ada3, ada3, ada3, w_in, w_in, w_out, ln_g, ln_b)


def _inproj_kernel(x_ref, sh_ref, sc_ref, w_ref, cs_ref, wgate_ref, z_ref, zg_ref, h_ref):
    j = pl.program_id(2)

    def project(h):
        z = jnp.dot(h, w_ref[...], preferred_element_type=F32)
        z_ref[0] = (z * cs_ref[...]).astype(BF16)

    @pl.when(j == 0)
    def _():
        half = x_ref.shape[1] // 2
        for r in range(2):
            rows = pl.ds(r * half, half)
            h = (_ln(x_ref[0, rows, :]) * (1.0 + sc_ref[0]) + sh_ref[0]).astype(BF16)
            h_ref[rows, :] = h
            zg_ref[0, rows, :] = jnp.dot(h, wgate_ref[...], preferred_element_type=F32)
            z_ref[0, rows, :] = (jnp.dot(h, w_ref[...], preferred_element_type=F32) * cs_ref[...]).astype(BF16)

    @pl.when(j > 0)
    def _():
        project(h_ref[...])


def _inproj(x, ada3, idx, w_mix, w_gate):
    b, s, d = x.shape
    n = w_mix.shape[1]
    tm = min(INPROJ_ROW_TILE, s)
    tn = INPROJ_COL_TILE
    vec = lambda j: pl.BlockSpec((1, 1, d), lambda bi, i, jj: (bi, 0, j))
    col_scale = jnp.where(jnp.arange(n) < FOX_WIDTH, FOX_HEAD_DIM ** -0.5 * LOG2E, 1.0).astype(F32).reshape(1, n)
    return pl.pallas_call(
        _inproj_kernel,
        grid=(b, s // tm, n // tn),
        in_specs=[pl.BlockSpec((1, tm, d), lambda bi, i, j: (bi, i, 0)),
                  vec(idx), vec(idx + 1),
                  pl.BlockSpec((d, tn), lambda bi, i, j: (0, j)),
                  pl.BlockSpec((1, tn), lambda bi, i, j: (0, j)),
                  pl.BlockSpec((d, GATE_LANES), lambda bi, i, j: (0, 0))],
        out_specs=[pl.BlockSpec((1, tm, tn), lambda bi, i, j: (bi, i, j)),
                   pl.BlockSpec((1, tm, GATE_LANES), lambda bi, i, j: (bi, i, 0))],
        out_shape=[jax.ShapeDtypeStruct((b, s, n), BF16),
                   jax.ShapeDtypeStruct((b, s, GATE_LANES), F32)],
        scratch_shapes=[pltpu.VMEM((tm, d), BF16)],
        compiler_params=_cparams(("parallel", "parallel", "arbitrary")),
        name="in_proj",
    )(x, ada3, ada3, w_mix, col_scale, w_gate)


def _split3(v):
    hi = v.astype(BF16)
    r = v - hi.astype(F32)
    mid = r.astype(BF16)
    lo = (r - mid.astype(F32)).astype(BF16)
    return hi, mid, lo


def _gate_kernel(zg_ref, bias_ref, nat_ref, t_ref, kx_ref, carry_ref, *, chunk):
    i = pl.program_id(1)

    @pl.when(i == 0)
    def _():
        carry_ref[...] = jnp.zeros_like(carry_ref)

    rows = chunk
    col = lax.broadcasted_iota(jnp.int32, (rows, GATE_LANES), 1)
    is_in_gate = (col >= G_MI) & (col < G_MF)
    r_i = lax.broadcasted_iota(jnp.int32, (rows, rows), 0)
    c_i = lax.broadcasted_iota(jnp.int32, (rows, rows), 1)
    tri = jnp.where(r_i >= c_i, 1.0, 0.0).astype(BF16)
    for c in range(zg_ref.shape[1] // rows):
        span = pl.ds(c * rows, rows)
        z = zg_ref[0, span, :] + bias_ref[...]
        v = jnp.where(is_in_gate, z, _log_sigmoid(z))
        hi, mid, lo = _split3(v)
        csum = (jnp.dot(tri, hi, preferred_element_type=F32)
                + jnp.dot(tri, mid, preferred_element_type=F32)
                + jnp.dot(tri, lo, preferred_element_type=F32))
        running = csum + carry_ref[...]
        carry_ref[...] = running[rows - 1:rows, :]
        out = jnp.where(col < G_MI, running, jnp.where(is_in_gate, v, csum))
        nat_ref[0, span, :] = out
        t_ref[0, :, span] = out.T[:G_ROWS, :]
        for h in range(N_FOX_HEADS):
            parts = _split3(jnp.broadcast_to(running[:, h:h + 1] * (-LOG2E), z.shape))
            tile = jnp.zeros(z.shape, F32)
            for t in reversed(range(FOX_AUG)):
                tile = jnp.where(col == t, parts[t].astype(F32), tile)
            kx_ref[0, span, h * GATE_LANES:(h + 1) * GATE_LANES] = tile.astype(BF16)


def _gates(zg, bias):
    b, s, _ = zg.shape
    chunk = min(MLSTM_CHUNK, s)
    tl = min(GATE_TILE, s)
    return pl.pallas_call(
        functools.partial(_gate_kernel, chunk=chunk),
        grid=(b, s // tl),
        in_specs=[pl.BlockSpec((1, tl, GATE_LANES), lambda bi, i: (bi, i, 0)),
                  pl.BlockSpec((1, GATE_LANES), lambda bi, i: (0, 0))],
        out_specs=[pl.BlockSpec((1, tl, GATE_LANES), lambda bi, i: (bi, i, 0)),
                   pl.BlockSpec((1, G_ROWS, tl), lambda bi, i: (bi, 0, i)),
                   pl.BlockSpec((1, tl, N_FOX_HEADS * GATE_LANES), lambda bi, i: (bi, i, 0))],
        out_shape=[jax.ShapeDtypeStruct((b, s, GATE_LANES), F32),
                   jax.ShapeDtypeStruct((b, G_ROWS, s), F32),
                   jax.ShapeDtypeStruct((b, s, N_FOX_HEADS * GATE_LANES), BF16)],
        scratch_shapes=[pltpu.VMEM((1, GATE_LANES), F32)],
        compiler_params=_cparams(("parallel", "arbitrary")),
        name="gates",
    )(zg, bias)


def _fox_kernel(q_ref, k_ref, kx_ref, v_ref, o_ref, s_buf, p_buf, a_buf, m_ref, acc_ref, *, tq, tk):
    qi = pl.program_id(2)
    dh = FOX_HEAD_DIM
    n_diag = tq // tk
    assert n_diag % 2 == 0
    n = qi * n_diag
    lane_q = lax.broadcasted_iota(jnp.int32, (tq, dh), 1)
    lane_k = lax.broadcasted_iota(jnp.int32, (tk, dh), 1)
    q = jnp.concatenate([q_ref[0], jnp.where(lane_q < FOX_AUG, 1.0, 0.0).astype(BF16)], axis=1)
    v_ones = jnp.where(lane_k == 0, 1.0, 0.0).astype(BF16)

    def logits(slot, j):
        start = pl.multiple_of(j * tk, tk)
        k = jnp.concatenate([k_ref[0, pl.ds(start, tk), :], kx_ref[0, pl.ds(start, tk), :]], axis=1)
        s_buf[slot] = lax.dot_general(q, k, (((1,), (1,)), ((), ())), preferred_element_type=F32)

    def softmax(slot, diag):
        s = s_buf[slot]
        if diag is not None:
            r_i = lax.broadcasted_iota(jnp.int32, s.shape, 0)
            c_i = lax.broadcasted_iota(jnp.int32, s.shape, 1)
            s = jnp.where(c_i + diag * tk <= r_i, s, NEG_BIG)
        m = m_ref[...]
        m_new = jnp.maximum(m, jnp.broadcast_to(jnp.max(s, axis=-1, keepdims=True), m.shape))
        for c in range(tk // dh):
            p_buf[slot, :, c * dh:(c + 1) * dh] = jnp.exp2(s[:, c * dh:(c + 1) * dh] - m_new).astype(BF16)
        a_buf[slot] = jnp.exp2(m - m_new)
        m_ref[...] = m_new

    def accumulate(slot, j):
        start = pl.multiple_of(j * tk, tk)
        v = jnp.concatenate([v_ref[0, pl.ds(start, tk), :], v_ones], axis=1)
        pv = jnp.dot(p_buf[slot], v, preferred_element_type=F32)
        a = a_buf[slot]
        for c in range(2):
            acc_ref[:, c * dh:(c + 1) * dh] = a * acc_ref[:, c * dh:(c + 1) * dh] + pv[:, c * dh:(c + 1) * dh]

    def block_of(t):
        if isinstance(t, int):
            return n + t if t < n_diag else t - n_diag
        return jnp.where(t < n_diag, n + t, t - n_diag)

    m_ref[...] = jnp.full(m_ref.shape, NEG_BIG, F32)
    acc_ref[...] = jnp.zeros(acc_ref.shape, F32)
    logits(0, block_of(0))
    for t in range(n_diag):
        logits(1 - t % 2, block_of(t + 1))
        softmax(t % 2, t)
        if t > 0:
            accumulate(1 - t % 2, block_of(t - 1))

    def pair(i):
        t = 2 * i
        softmax(0, None)
        logits(1, block_of(t + 1))
        accumulate(1, block_of(t - 1))
        softmax(1, None)
        logits(0, block_of(t + 2))
        accumulate(0, block_of(t))

    first, n_pairs = n_diag // 2, n // 2

    def trip(g, carry):
        for u in range(FOX_PAIRS_PER_TRIP):
            pair(first + FOX_PAIRS_PER_TRIP * g + u)
        return carry

    n_trips = n_pairs // FOX_PAIRS_PER_TRIP
    lax.fori_loop(0, n_trips, trip, 0)
    done = n_trips * FOX_PAIRS_PER_TRIP
    size = FOX_PAIRS_PER_TRIP // 2
    while size >= 1:
        @pl.when((n_pairs & size) != 0)
        def _(done=done, size=size):
            for u in range(size):
                pair(first + done + u)
        done = done + (n_pairs & size)
        size //= 2

    n_steps = n + n_diag
    accumulate(1, block_of(n_steps - 1))
    acc = acc_ref[...]
    o_ref[0] = (acc[:, :dh] / acc[:, dh:dh + 1]).astype(o_ref.dtype)


def _fox(z, kx):
    b, s, _ = z.shape
    tq = min(FOX_Q_TILE, s)
    tk = min(FOX_K_TILE, tq // 2)
    h = N_FOX_HEADS
    return pl.pallas_call(
        functools.partial(_fox_kernel, tq=tq, tk=tk),
        grid=(b, h, s // tq),
        in_specs=[pl.BlockSpec((1, tq, FOX_HEAD_DIM), lambda bi, hi, qi: (bi, qi, hi)),
                  pl.BlockSpec((1, s, FOX_HEAD_DIM), lambda bi, hi, qi: (bi, 0, h + hi)),
                  pl.BlockSpec((1, s, GATE_LANES), lambda bi, hi, qi: (bi, 0, hi)),
                  pl.BlockSpec((1, s, FOX_HEAD_DIM), lambda bi, hi, qi: (bi, 0, 2 * h + hi))],
        out_specs=pl.BlockSpec((1, tq, FOX_HEAD_DIM), lambda bi, hi, qi: (bi, qi, hi)),
        out_shape=jax.ShapeDtypeStruct((b, s, FOX_WIDTH), BF16),
        scratch_shapes=[pltpu.VMEM((2, tq, tk), F32),
                        pltpu.VMEM((2, tq, tk), BF16),
                        pltpu.VMEM((2, tq, FOX_HEAD_DIM), F32),
                        pltpu.VMEM((tq, FOX_HEAD_DIM), F32),
                        pltpu.VMEM((tq, 2 * FOX_HEAD_DIM), F32)],
        compiler_params=_cparams(("parallel", "parallel", "arbitrary")),
        name="fox_attn",
    )(z, z, kx, z)


def _mlstm_kernel(qk_ref, v_ref, o_ref, gn_ref, gt_ref, cw_ref, cb_ref, ng_ref, y_ref,
                  c_st, n_st, m_st, tail, ubuf):
    c = pl.program_id(0)
    L = qk_ref.shape[1]

    @pl.when(c == 0)
    def _():
        c_st[...] = jnp.zeros_like(c_st)
        n_st[...] = jnp.zeros_like(n_st)
        m_st[...] = jnp.zeros_like(m_st)
        tail[...] = jnp.zeros_like(tail)

    r_i = lax.broadcasted_iota(jnp.int32, (L, L), 0)
    c_i = lax.broadcasted_iota(jnp.int32, (L, L), 1)
    causal = c_i <= r_i
    _mlstm_chunk(causal, qk_ref, v_ref, o_ref, gn_ref, gt_ref, cw_ref, cb_ref, ng_ref, y_ref,
                 c_st, n_st, m_st, tail, ubuf)


def _mlstm_chunk(causal, qk_ref, v_ref, o_ref, gn_ref, gt_ref, cw_ref, cb_ref, ng_ref, y_ref,
                 c_st, n_st, m_st, tail, ubuf):
    nb, L = qk_ref.shape[0], qk_ref.shape[1]
    dk, dv = MLSTM_QK_DIM, MLSTM_V_DIM
    qk = []
    for bi in range(nb):
        u = qk_ref[bi].astype(F32)
        ubuf[bi, 0:8, :] = tail[bi]
        ubuf[bi, 8:8 + L, :] = u
        tail[bi] = u[L - 8:L, :]
        conv = cb_ref[...] + cw_ref[3:4, :] * u
        for d in range(1, CONV_WIDTH):
            conv = conv + cw_ref[3 - d:4 - d, :] * ubuf[bi, 8 - d:8 - d + L, :]
        qk.append(conv * _sigmoid(conv))
    gn = [gn_ref[bi] for bi in range(nb)]
    gt = [gt_ref[bi] for bi in range(nb)]

    heads = range(nb * N_MLSTM_HEADS)
    bh = [divmod(i, N_MLSTM_HEADS) for i in heads]
    qf = [qk[bi][:, h * dk:(h + 1) * dk] for bi, h in bh]
    q = [x.astype(BF16) for x in qf]
    kf = [qk[bi][:, MLSTM_QK_WIDTH + h * dk:MLSTM_QK_WIDTH + (h + 1) * dk] * (dk ** -0.5) for bi, h in bh]
    v = [v_ref[bi, :, h * dv:(h + 1) * dv] for bi, h in bh]
    bcol = [gn[bi][:, G_MF + h:G_MF + h + 1] for bi, h in bh]
    icol = [gn[bi][:, G_MI + h:G_MI + h + 1] for bi, h in bh]
    brow = [gt[bi][G_MF + h:G_MF + h + 1, :] for bi, h in bh]
    irow = [gt[bi][G_MI + h:G_MI + h + 1, :] for bi, h in bh]
    m_prev = [m_st[i][:, 0:1] for i in heads]
    c_prev = [c_st[i] for i in heads]
    n_prev = [n_st[i] for i in heads]

    m_inter = [bcol[h] + m_prev[h] for h in heads]
    dlog = [jnp.where(causal, bcol[h] - brow[h] + irow[h], NEG_BIG) for h in heads]
    m_t = [jnp.maximum(m_inter[h], jnp.max(dlog[h], axis=-1, keepdims=True)) for h in heads]
    s = [lax.dot_general(q[h], kf[h].astype(BF16), (((1,), (1,)), ((), ())), preferred_element_type=F32)
         for h in heads]
    s = [s[h] * jnp.exp(dlog[h] - m_t[h]) for h in heads]
    inter = [jnp.exp(m_inter[h] - m_t[h]) for h in heads]
    num = [inter[h] * jnp.dot(q[h], c_prev[h].astype(BF16), preferred_element_type=F32)
           + jnp.dot(s[h].astype(BF16), v[h], preferred_element_type=F32) for h in heads]
    den = [inter[h] * jnp.sum(qf[h] * n_prev[h], axis=-1, keepdims=True) + jnp.sum(s[h], axis=-1, keepdims=True)
           for h in heads]
    hh = [num[h] / jnp.maximum(jnp.abs(den[h]), jnp.exp(-m_t[h])) for h in heads]

    b_last = [bcol[h][L - 1:L, :] for h in heads]
    wlog = [b_last[h] - bcol[h] + icol[h] for h in heads]
    m_new = [jnp.maximum(b_last[h] + m_prev[h], jnp.max(wlog[h], axis=0, keepdims=True)) for h in heads]
    decay = [jnp.exp(b_last[h] + m_prev[h] - m_new[h]) for h in heads]
    wk = [kf[h] * jnp.exp(wlog[h] - m_new[h]) for h in heads]
    for i in heads:
        c_st[i] = decay[i] * c_prev[i] + jnp.dot(wk[i].T.astype(BF16), v[i], preferred_element_type=F32)
        n_st[i] = decay[i] * n_prev[i] + jnp.sum(wk[i], axis=0, keepdims=True)
        m_st[i] = jnp.broadcast_to(m_new[i], (1, GATE_LANES))

    hn = [hh[i] * lax.rsqrt(jnp.mean(hh[i] * hh[i], axis=-1, keepdims=True) + LN_EPS) for i in heads]
    for i, (bi, h) in zip(heads, bh):
        og = _sigmoid(o_ref[bi, :, h * dv:(h + 1) * dv].astype(F32))
        y_ref[bi, :, h * dv:(h + 1) * dv] = (og * (hn[i] * ng_ref[:, h * dv:(h + 1) * dv])).astype(y_ref.dtype)


def _mlstm(z, gates_n, gates_t, conv_w, conv_b, norm_g):
    b, s, _ = z.shape
    L = min(MLSTM_CHUNK, s)
    wq = 2 * MLSTM_QK_WIDTH
    assert wq == MLSTM_WIDTH == FOX_WIDTH
    base = 3 * FOX_WIDTH // wq
    return pl.pallas_call(
        _mlstm_kernel,
        grid=(s // L,),
        in_specs=[pl.BlockSpec((b, L, wq), lambda ci: (0, ci, base)),
                  pl.BlockSpec((b, L, MLSTM_WIDTH), lambda ci: (0, ci, base + 1)),
                  pl.BlockSpec((b, L, MLSTM_WIDTH), lambda ci: (0, ci, base + 2)),
                  pl.BlockSpec((b, L, GATE_LANES), lambda ci: (0, ci, 0)),
                  pl.BlockSpec((b, G_ROWS, L), lambda ci: (0, 0, ci)),
                  pl.BlockSpec((CONV_WIDTH, wq), lambda ci: (0, 0)),
                  pl.BlockSpec((1, wq), lambda ci: (0, 0)),
                  pl.BlockSpec((1, MLSTM_WIDTH), lambda ci: (0, 0))],
        out_specs=pl.BlockSpec((b, L, MLSTM_WIDTH), lambda ci: (0, ci, 0)),
        out_shape=jax.ShapeDtypeStruct((b, s, MLSTM_WIDTH), BF16),
        scratch_shapes=[pltpu.VMEM((b * N_MLSTM_HEADS, MLSTM_QK_DIM, MLSTM_V_DIM), F32),
                        pltpu.VMEM((b * N_MLSTM_HEADS, 1, MLSTM_QK_DIM), F32),
                        pltpu.VMEM((b * N_MLSTM_HEADS, 1, GATE_LANES), F32),
                        pltpu.VMEM((b, 8, wq), F32),
                        pltpu.VMEM((b, L + 8, wq), F32)],
        compiler_params=_cparams(("arbitrary",)),
        name="mlstm",
    )(z, z, z, gates_n, gates_t, conv_w, conv_b, norm_g)


def _outproj_kernel(yf_ref, ym_ref, x_ref, gt_ref, w_ref, lng_ref, lnb_ref, o_ref):
    nf = yf_ref.shape[2]
    half = x_ref.shape[1] // 2
    halves = [pl.ds(r * half, half) for r in range(2)]
    hmix = [jnp.dot(yf_ref[0, rows, :], w_ref[0:nf, :], preferred_element_type=F32)
            + jnp.dot(ym_ref[0, rows, :], w_ref[nf:, :], preferred_element_type=F32) for rows in halves]
    for rows, hm in zip(halves, hmix):
        y = ALPHA * x_ref[0, rows, :] + (1.0 + gt_ref[0]) * hm
        o_ref[0, rows, :] = _ln(y) * lng_ref[...] + lnb_ref[...]


def _outproj(y_fox, y_mlstm, x, ada3, idx, w_out, ln_g, ln_b):
    b, s, d = x.shape
    tm = min(OUTPROJ_ROW_TILE, s)
    return pl.pallas_call(
        _outproj_kernel,
        grid=(b, s // tm),
        in_specs=[pl.BlockSpec((1, tm, FOX_WIDTH), lambda bi, i: (bi, i, 0)),
                  pl.BlockSpec((1, tm, MLSTM_WIDTH), lambda bi, i: (bi, i, 0)),
                  pl.BlockSpec((1, tm, d), lambda bi, i: (bi, i, 0)),
                  pl.BlockSpec((1, 1, d), lambda bi, i: (bi, 0, idx)),
                  pl.BlockSpec(w_out.shape, lambda bi, i: (0, 0)),
                  pl.BlockSpec((1, d), lambda bi, i: (0, 0)),
                  pl.BlockSpec((1, d), lambda bi, i: (0, 0))],
        out_specs=pl.BlockSpec((1, tm, d), lambda bi, i: (bi, i, 0)),
        out_shape=jax.ShapeDtypeStruct((b, s, d), F32),
        compiler_params=_cparams(("parallel", "parallel")),
        name="out_proj",
    )(y_fox, y_mlstm, x, ada3, w_out, ln_g, ln_b)


def _layer(x, c, w_ada, b_ada, ffn1_w_in, ffn1_w_out, ln1_g, ln1_b, w_in, fox_f_bias, mlstm_conv_w,
           mlstm_conv_b, mlstm_i_bias, mlstm_f_bias, mlstm_norm_g, w_out, ln2_g, ln2_b,
           ffn2_w_in, ffn2_w_out, ln3_g, ln3_b):
    b, s, d = x.shape
    row = lambda a: a.reshape(1, -1)

    c_pad = jnp.zeros((8, d), F32).at[:b].set(c)
    ada3 = _ada(c_pad, w_ada, row(b_ada)).reshape(8, 1, N_ADA * d)

    x = _ffn(x, ada3, 0, ffn1_w_in.astype(BF16), ffn1_w_out.astype(BF16), row(ln1_g), row(ln1_b))

    w_in = w_in.astype(BF16)
    w_mix = jnp.concatenate([w_in[:, :COL_FOX_F], w_in[:, COL_MLSTM_Q:COL_MLSTM_I], w_in[:, COL_MLSTM_O:]], axis=1)
    w_gate = jnp.concatenate([w_in[:, COL_FOX_F:COL_MLSTM_Q], w_in[:, COL_MLSTM_I:COL_MLSTM_O],
                              jnp.zeros((d, GATE_LANES - G_ROWS), BF16)], axis=1)
    gate_bias = jnp.concatenate([fox_f_bias, mlstm_i_bias, mlstm_f_bias,
                                 jnp.zeros((GATE_LANES - G_ROWS,), F32)]).reshape(1, GATE_LANES)
    z, zg = _inproj(x, ada3, 3, w_mix, w_gate)
    gates_n, gates_t, kx = _gates(zg, gate_bias)
    y_fox = _fox(z, kx)
    y_mlstm = _mlstm(z, gates_n, gates_t, mlstm_conv_w, row(mlstm_conv_b), row(mlstm_norm_g))
    x = _outproj(y_fox, y_mlstm, x, ada3, 5, w_out.astype(BF16), row(ln2_g), row(ln2_b))

    x = _ffn(x, ada3, 6, ffn2_w_in.astype(BF16), ffn2_w_out.astype(BF16), row(ln3_g), row(ln3_b))
    return x


def kernel(x, c, w_ada, b_ada, ffn1_w_in, ffn1_w_out, ln1_g, ln1_b, w_in, fox_f_bias, mlstm_conv_w,
           mlstm_conv_b, mlstm_i_bias, mlstm_f_bias, mlstm_norm_g, w_out, ln2_g, ln2_b,
           ffn2_w_in, ffn2_w_out, ln3_g, ln3_b):
    for l in range(DEPTH):
        x = _layer(x, c, w_ada[l], b_ada[l], ffn1_w_in[l], ffn1_w_out[l], ln1_g[l], ln1_b[l],
                   w_in[l], fox_f_bias[l], mlstm_conv_w[l], mlstm_conv_b[l], mlstm_i_bias[l],
                   mlstm_f_bias[l], mlstm_norm_g[l], w_out[l], ln2_g[l], ln2_b[l],
                   ffn2_w_in[l], ffn2_w_out[l], ln3_g[l], ln3_b[l])
    return x
```
